```python
import math
import jax, jax.numpy as jnp
from jax import lax
import numpy as np

D_MODEL = 2048
BATCH = 4
SEQ = 8192
DEPTH = 2
DEC_BATCH = 8
DEC_SEQ = 32
PAST_LEN = 1024

CHUNK = 64
BRANCH_WIDTH = D_MODEL // 2
N_BRANCH = 3
A_WIDTH = BRANCH_WIDTH
A_BLOCK = 128
A_GROUP = 128
A_GROUPS = A_WIDTH // A_GROUP
B_HEADS = 4
B_DK = D_MODEL // 16
B_DV = D_MODEL // 8
B_QK = B_HEADS * B_DK
B_V = B_HEADS * B_DV
ROPE_BASE = 10000.0
C_HEADS = 8
C_DH = D_MODEL // 32
C_DV = 2 * C_DH
C_QK = C_HEADS * 2 * C_DH
C_V = C_HEADS * C_DV
Q_BLOCK = 128
N_GROUPS = 4
EXPERTS_PER_GROUP = 4
N_EXPERTS = N_GROUPS * EXPERTS_PER_GROUP
TOP_K_IN_GROUP = 2
D_EXPERT = D_MODEL // 4
EPS = 1e-6

SPLIT_SIZES = (A_WIDTH, A_WIDTH, B_QK, B_QK, B_V, B_V, C_QK, C_QK, C_V, N_BRANCH * D_MODEL)
D_IN = sum(SPLIT_SIZES)

kernel_name = 'gated_hybrid_gmlp_retnet_diffattn_hmoe_step'


def _split_points():
    pts, acc = [], 0
    for size in SPLIT_SIZES[:-1]:
        acc += size
        pts.append(acc)
    return pts


def rms_norm(x, w):
    xf = x.astype(jnp.float32)
    y = xf * lax.rsqrt(jnp.mean(xf * xf, axis=-1, keepdims=True) + EPS)
    return (y * w.astype(jnp.float32)).astype(x.dtype)


def rotary(x, pos):
    half = x.shape[-1] // 2
    inv_freq = 1.0 / (ROPE_BASE ** jnp.linspace(0.0, 1.0, half, dtype=jnp.float32))
    ang = pos.astype(jnp.float32)[:, None] * inv_freq[None, :]
    cos = jnp.cos(ang)[:, None, :]
    sin = jnp.sin(ang)[:, None, :]
    xf = x.astype(jnp.float32)
    x1, x2 = xf[..., :half], xf[..., half:]
    return jnp.concatenate([x1 * cos - x2 * sin, x2 * cos + x1 * sin], axis=-1).astype(x.dtype)


def gmlp_spatial(v, w_s, b_s):
    B, L, _ = v.shape
    nb = -(-L // A_BLOCK)
    pad = nb * A_BLOCK - L
    vb = jnp.pad(v, ((0, 0), (0, pad), (0, 0))).reshape(B, nb, A_BLOCK, A_GROUPS, A_GROUP)
    p = jnp.arange(A_BLOCK)
    mask = (p[None, :] // CHUNK) <= (p[:, None] // CHUNK)
    w = jnp.where(mask[None], w_s, 0.0).astype(v.dtype)
    s = jnp.einsum('gpq,bnqgc->bnpgc', w, vb) + b_s.T[None, None, :, :, None].astype(v.dtype)
    return s.reshape(B, nb * A_BLOCK, A_WIDTH)[:, :L]


def retention_log_decay():
    return jnp.log1p(-jnp.exp2(-5.0 - jnp.arange(B_HEADS, dtype=jnp.float32)))


def retention_chunk(q, k, v, s_prev):
    lg = retention_log_decay()
    L = q.shape[1]
    q, k, v = (t.astype(jnp.float32) for t in (q, k, v))
    idx = jnp.arange(L, dtype=jnp.float32)
    diff = idx[:, None] - idx[None, :]
    decay = jnp.where(diff >= 0, jnp.exp(jnp.maximum(diff, 0.0)[None] * lg[:, None, None]), 0.0)
    attn = jnp.einsum('bihd,bjhd->bhij', q, k) * decay[None]
    intra = jnp.einsum('bhij,bjhe->bihe', attn, v)
    q_decay = jnp.exp((idx + 1.0)[:, None] * lg[None, :])
    inter = jnp.einsum('bihd,bhde->bihe', q, s_prev) * q_decay[None, :, :, None]
    k_decay = jnp.exp((L - 1.0 - idx)[:, None] * lg[None, :])
    s_new = jnp.exp(L * lg)[None, :, None, None] * s_prev + jnp.einsum('bjhd,jh,bjhe->bhde', k, k_decay, v)
    return intra + inter, s_new


def retention_prompt(q, k, v):
    B, S = q.shape[:2]
    n = S // CHUNK

    def to_chunks(t):
        return jnp.moveaxis(t.reshape(B, n, CHUNK, *t.shape[2:]), 1, 0)

    def step(s, blk):
        o, s = retention_chunk(blk[0], blk[1], blk[2], s)
        return s, o

    s0 = jnp.zeros((B, B_HEADS, B_DK, B_DV), jnp.float32)
    s_fin, o = lax.scan(step, s0, (to_chunks(q), to_chunks(k), to_chunks(v)))
    return jnp.moveaxis(o, 0, 1).reshape(B, S, B_HEADS, B_DV), s_fin


def diff_attn_core(q, k, v, q_pos, k_pos, lam):
    s = jnp.einsum('bqhcd,bkhcd->bhcqk', q, k).astype(jnp.float32) * (C_DH ** -0.5)
    mask = (k_pos[None, :] // CHUNK) <= (q_pos[:, None] // CHUNK)
    p = jax.nn.softmax(jnp.where(mask, s, -jnp.inf), axis=-1)
    w = p[:, :, 0] - lam * p[:, :, 1]
    return jnp.einsum('bhqk,bkhe->bqhe', w.astype(v.dtype), v)


def diff_attn_prompt(q, k, v, lam):
    B, S = q.shape[:2]
    nq = S // Q_BLOCK
    qb = jnp.moveaxis(q.reshape(B, nq, Q_BLOCK, C_HEADS, 2, C_DH), 1, 0)
    q_pos = jnp.arange(S).reshape(nq, Q_BLOCK)
    k_pos = jnp.arange(S)
    out = lax.map(lambda blk: diff_attn_core(blk[0], k, v, blk[1], k_pos, lam), (qb, q_pos))
    return jnp.moveaxis(out, 0, 1).reshape(B, S, C_HEADS, C_DV)


def diff_attn_sample(q, k, v, cache_k, cache_v, lam):
    DB, L = q.shape[:2]
    P = cache_k.shape[1]
    k_all = jnp.concatenate([cache_k.reshape(DB, P, C_HEADS, 2, C_DH).astype(k.dtype), k], axis=1)
    v_all = jnp.concatenate([cache_v.astype(v.dtype), v], axis=1)
    return diff_attn_core(q, k_all, v_all, P + jnp.arange(L), jnp.arange(P + L), lam)


def hier_moe(xn, w_rg, b_rg, w_re, b_re, w_gate, w_up, w_down):
    B, L, _ = xn.shape
    lg = (xn @ w_rg).astype(jnp.float32) + b_rg.astype(jnp.float32)
    g_sel = jnp.argmax(lg, axis=-1)
    p_group = jnp.max(jax.nn.softmax(lg, axis=-1), axis=-1)
    le = ((xn @ w_re).astype(jnp.float32) + b_re.astype(jnp.float32)).reshape(B, L, N_GROUPS, EXPERTS_PER_GROUP)
    le_g = jnp.einsum('blg,blge->ble', jax.nn.one_hot(g_sel, N_GROUPS, dtype=jnp.float32), le)
    top_v, top_i = lax.top_k(le_g, TOP_K_IN_GROUP)
    pe = jax.nn.softmax(top_v, axis=-1) * p_group[..., None]
    expert_idx = g_sel[..., None] * EXPERTS_PER_GROUP + top_i
    combine = jnp.sum(jax.nn.one_hot(expert_idx, N_EXPERTS, dtype=jnp.float32) * pe[..., None], axis=-2).astype(xn.dtype)
    y = jnp.zeros_like(xn)
    for e in range(N_EXPERTS):
        h = jax.nn.silu(xn @ w_gate[e]) * (xn @ w_up[e])
        y = y + combine[..., e, None] * (h @ w_down[e])
    return y


def setup_inputs(seed: int = 0) -> dict:
    key = jax.random.key(seed)
    ks = jax.random.split(key, 26)
    f32 = jnp.float32

    def nrm(k, shape, scale):
        return scale * jax.random.normal(k, shape, f32)

    def gain(k, shape):
        return 1.0 + 0.01 * jax.random.normal(k, shape, f32)

    return {
        'x_prompt': nrm(ks[0], (BATCH, SEQ, D_MODEL), 1.0),
        'x_sample': nrm(ks[1], (DEC_BATCH, DEC_SEQ, D_MODEL), 1.0),
        'cache_k_c': nrm(ks[2], (DEPTH, DEC_BATCH, PAST_LEN, C_HEADS, 2 * C_DH), 1.0),
        'cache_v_c': nrm(ks[3], (DEPTH, DEC_BATCH, PAST_LEN, C_HEADS, C_DV), 1.0),
        'state_ret': nrm(ks[4], (DEPTH, DEC_BATCH, B_HEADS, B_DK, B_DV), 0.5),
        'norm_mix_w': gain(ks[5], (DEPTH, D_MODEL)),
        'w_in': nrm(ks[6], (DEPTH, D_MODEL, D_IN), D_MODEL ** -0.5),
        'a_norm_w': gain(ks[7], (DEPTH, A_WIDTH)),
        'a_ws': nrm(ks[8], (DEPTH, A_GROUPS, A_BLOCK, A_BLOCK), A_BLOCK ** -0.5),
        'a_bs': gain(ks[9], (DEPTH, A_GROUPS, A_BLOCK)),
        'b_norm_w': gain(ks[10], (DEPTH, B_DV)),
        'c_qnorm_w': gain(ks[11], (DEPTH, C_DH)),
        'c_knorm_w': gain(ks[12], (DEPTH, C_DH)),
        'c_lambda': nrm(ks[13], (DEPTH, 4, C_DH), 0.1),
        'c_subln_w': gain(ks[14], (DEPTH, C_DV)),
        'w_branch': nrm(ks[15], (DEPTH, N_BRANCH, BRANCH_WIDTH, D_MODEL), BRANCH_WIDTH ** -0.5),
        'w_out': nrm(ks[16], (DEPTH, D_MODEL, D_MODEL), D_MODEL ** -0.5),
        'norm_ffn_w': gain(ks[17], (DEPTH, D_MODEL)),
        'w_router_group': nrm(ks[18], (DEPTH, D_MODEL, N_GROUPS), D_MODEL ** -0.5),
        'b_router_group': nrm(ks[19], (DEPTH, N_GROUPS), 0.01),
        'w_router_expert': nrm(ks[20], (DEPTH, D_MODEL, N_EXPERTS), D_MODEL ** -0.5),
        'b_router_expert': nrm(ks[21], (DEPTH, N_EXPERTS), 0.01),
        'w_gate_e': nrm(ks[22], (DEPTH, N_EXPERTS, D_MODEL, D_EXPERT), D_MODEL ** -0.5),
        'w_up_e': nrm(ks[23], (DEPTH, N_EXPERTS, D_MODEL, D_EXPERT), D_MODEL ** -0.5),
        'w_down_e': nrm(ks[24], (DEPTH, N_EXPERTS, D_EXPERT, D_MODEL), D_EXPERT ** -0.5),
    }


def reference(x_prompt, x_sample, cache_k_c, cache_v_c, state_ret, norm_mix_w, w_in, a_norm_w, a_ws, a_bs,
              b_norm_w, c_qnorm_w, c_knorm_w, c_lambda, c_subln_w, w_branch, w_out, norm_ffn_w,
              w_router_group, b_router_group, w_router_expert, b_router_expert, w_gate_e, w_up_e, w_down_e):

    def mixer(xn, pos, l, ret_state, kv_cache):
        B, L, _ = xn.shape
        lam_init = 0.8 - 0.6 * math.exp(-0.3 * l)
        lq = c_lambda[l].astype(jnp.float32)
        lam = jnp.exp(jnp.sum(lq[0] * lq[1])) - jnp.exp(jnp.sum(lq[2] * lq[3])) + lam_init
        a_u, a_v, b_q, b_k, b_v, b_g, c_q, c_k, c_v, gates = jnp.split(xn @ w_in[l], _split_points(), axis=-1)
        a_u = jax.nn.gelu(a_u)
        a_v = rms_norm(jax.nn.gelu(a_v), a_norm_w[l])
        a_y = a_u * gmlp_spatial(a_v, a_ws[l], a_bs[l])
        b_q = rotary(b_q.reshape(B, L, B_HEADS, B_DK), pos)
        b_k = rotary(b_k.reshape(B, L, B_HEADS, B_DK), pos) * (B_DK ** -0.5)
        b_v = b_v.reshape(B, L, B_HEADS, B_DV)
        if ret_state is None:
            b_o, s_new = retention_prompt(b_q, b_k, b_v)
        else:
            b_o, s_new = retention_chunk(b_q, b_k, b_v, ret_state.astype(jnp.float32))
        b_y = rms_norm(b_o.astype(xn.dtype), b_norm_w[l]).reshape(B, L, B_V) * jax.nn.silu(b_g)
        c_q = rms_norm(c_q.reshape(B, L, C_HEADS, 2, C_DH), c_qnorm_w[l])
        c_k = rms_norm(c_k.reshape(B, L, C_HEADS, 2, C_DH), c_knorm_w[l])
        c_v = c_v.reshape(B, L, C_HEADS, C_DV)
        if kv_cache is None:
            c_o = diff_attn_prompt(c_q, c_k, c_v, lam)
        else:
            c_o = diff_attn_sample(c_q, c_k, c_v, kv_cache[0], kv_cache[1], lam)
        c_y = (rms_norm(c_o, c_subln_w[l]) * (1.0 - lam_init)).reshape(B, L, C_V)
        g = jax.nn.sigmoid(gates.reshape(B, L, N_BRANCH, D_MODEL))
        up = jnp.einsum('blnc,ncd->blnd', jnp.stack([a_y, b_y, c_y], axis=2), w_branch[l])
        y = jnp.sum(g * up, axis=2) @ w_out[l]
        return y, c_k.reshape(B, L, C_HEADS, 2 * C_DH), c_v, s_new, a_v

    def ffn(x, l):
        return hier_moe(rms_norm(x, norm_ffn_w[l]), w_router_group[l], b_router_group[l], w_router_expert[l],
                        b_router_expert[l], w_gate_e[l], w_up_e[l], w_down_e[l])

    past = cache_k_c.shape[2]
    pos_p = jnp.arange(x_prompt.shape[1])
    pos_s = past + jnp.arange(x_sample.shape[1])
    yp, ys = x_prompt, x_sample
    kp_l, vp_l, rp_l, ks_l, vs_l, rs_l, as_l = [], [], [], [], [], [], []
    for l in range(DEPTH):
        h, kp, vp, rp, _ = mixer(rms_norm(yp, norm_mix_w[l]), pos_p, l, None, None)
        yp = yp + h
        yp = yp + ffn(yp, l)
        h, kn, vn, rn, avn = mixer(rms_norm(ys, norm_mix_w[l]), pos_s, l, state_ret[l], (cache_k_c[l], cache_v_c[l]))
        ys = ys + h
        ys = ys + ffn(ys, l)
        kp_l.append(kp)
        vp_l.append(vp)
        rp_l.append(rp.astype(state_ret.dtype))
        ks_l.append(kn)
        vs_l.append(vn)
        rs_l.append(rn.astype(state_ret.dtype))
        as_l.append(avn)
    new_k_prompt = jnp.stack(kp_l, 0)
    new_v_prompt = jnp.stack(vp_l, 0)
    ret_prompt = jnp.stack(rp_l, 0)
    new_k_sample = jnp.stack(ks_l, 0)
    new_v_sample = jnp.stack(vs_l, 0)
    ret_sample = jnp.stack(rs_l, 0)
    gmlp_v_sample = jnp.stack(as_l, 0)
    return (yp, ys, new_k_prompt, new_v_prompt, ret_prompt, new_k_sample, new_v_sample, ret_sample, gmlp_v_sample)
```

```python
import functools
import math

import jax
import jax.numpy as jnp
from jax import lax
from jax.experimental import pallas as pl
from jax.experimental.pallas import tpu as pltpu

F32 = jnp.float32
BF16 = jnp.bfloat16

D_MODEL = 2048
CHUNK = 64
A_WIDTH = 1024
A_BLOCK = 128
A_GROUP = 128
A_GROUPS = 8
B_HEADS = 4
B_DK = 128
B_DV = 256
ROPE_BASE = 10000.0
C_HEADS = 8
C_DH = 64
C_DV = 128
N_GROUPS = 4
EXPERTS_PER_GROUP = 4
N_EXPERTS = 16
D_EXPERT = 512
EPS = 1e-6
D_IN = 14336

OFF_AU, OFF_AV, OFF_BQ, OFF_BK, OFF_BV, OFF_BG, OFF_CQ, OFF_CK, OFF_CV, OFF_GATE = (
    0, 1024, 2048, 2560, 3072, 4096, 5120, 6144, 7168, 8192)

LANES = 128
MIB = 1024 * 1024
NEG_BIG = -1e30


def _cparams(sem, vmem_mib):
    return pltpu.CompilerParams(dimension_semantics=sem, vmem_limit_bytes=vmem_mib * MIB)


def _rms(x, w):
    ms = jnp.mean(x * x, axis=-1, keepdims=True)
    return x * lax.rsqrt(ms + EPS) * w


def _pick(n, pref):
    t = min(pref, n)
    while n % t:
        t //= 2
    return t


def _inproj_kernel(x_ref, g_ref, w_ref, o_ref, xn_ref):
    @pl.when(pl.program_id(1) == 0)
    def _():
        xn_ref[...] = _rms(x_ref[...], g_ref[...]).astype(BF16)

    o_ref[...] = jnp.dot(xn_ref[...], w_ref[...], preferred_element_type=F32).astype(o_ref.dtype)


def _inproj(x2, g, w_bf):
    n = x2.shape[0]
    tm = _pick(n, 1024)
    tn = 1024
    return pl.pallas_call(
        _inproj_kernel,
        grid=(n // tm, D_IN // tn),
        in_specs=[pl.BlockSpec((tm, D_MODEL), lambda i, j: (i, 0)),
                  pl.BlockSpec((1, D_MODEL), lambda i, j: (0, 0)),
                  pl.BlockSpec((D_MODEL, tn), lambda i, j: (0, j))],
        out_specs=pl.BlockSpec((tm, tn), lambda i, j: (i, j)),
        out_shape=jax.ShapeDtypeStruct((n, D_IN), BF16),
        scratch_shapes=[pltpu.VMEM((tm, D_MODEL), BF16)],
        compiler_params=_cparams(("parallel", "arbitrary"), 48),
    )(x2, g.reshape(1, D_MODEL), w_bf)


def _gmlp_kernel(u_ref, v_ref, nw_ref, ws_ref, bs_ref, y_ref, *rest, pb, nblk, emit_v):
    u = jax.nn.gelu(u_ref[...].astype(F32))
    v = _rms(jax.nn.gelu(v_ref[...].astype(F32)), nw_ref[...])
    if emit_v:
        rest[0][...] = v
    vb = v.astype(BF16)
    p = lax.broadcasted_iota(jnp.int32, (A_BLOCK, A_BLOCK), 0)
    q = lax.broadcasted_iota(jnp.int32, (A_BLOCK, A_BLOCK), 1)
    mask = (q // CHUNK) <= (p // CHUNK)
    for g in range(A_GROUPS):
        w = jnp.where(mask, ws_ref[g], 0.0).astype(BF16)
        bias = bs_ref[g, :pb, :]
        cs = slice(g * A_GROUP, (g + 1) * A_GROUP)
        for b in range(nblk):
            rs = slice(b * pb, (b + 1) * pb)
            vg = vb[rs, cs]
            if pb < A_BLOCK:
                vg = jnp.concatenate([vg, jnp.zeros((A_BLOCK - pb, A_GROUP), BF16)], axis=0)
            s = jnp.dot(w, vg, preferred_element_type=F32)[:pb] + bias
            y_ref[rs, cs] = (u[rs, cs] * s).astype(y_ref.dtype)


def _gmlp(proj2, seq_len, nw, ws, bs_b, emit_v):
    n = proj2.shape[0]
    if seq_len % A_BLOCK == 0:
        pb, nblk = A_BLOCK, 4 if seq_len % (4 * A_BLOCK) == 0 else 1
    else:
        assert seq_len <= A_BLOCK
        pb, nblk = seq_len, 1
    r = pb * nblk
    out_shape = [jax.ShapeDtypeStruct((n, A_WIDTH), BF16)]
    out_specs = [pl.BlockSpec((r, A_WIDTH), lambda i: (i, 0))]
    if emit_v:
        out_shape.append(jax.ShapeDtypeStruct((n, A_WIDTH), F32))
        out_specs.append(pl.BlockSpec((r, A_WIDTH), lambda i: (i, 0)))
    res = pl.pallas_call(
        functools.partial(_gmlp_kernel, pb=pb, nblk=nblk, emit_v=emit_v),
        grid=(n // r,),
        in_specs=[pl.BlockSpec((r, A_WIDTH), lambda i: (i, OFF_AU // A_WIDTH)),
                  pl.BlockSpec((r, A_WIDTH), lambda i: (i, OFF_AV // A_WIDTH)),
                  pl.BlockSpec((1, A_WIDTH), lambda i: (0, 0)),
                  pl.BlockSpec((A_GROUPS, A_BLOCK, A_BLOCK), lambda i: (0, 0, 0)),
                  pl.BlockSpec((A_GROUPS, A_BLOCK, A_GROUP), lambda i: (0, 0, 0))],
        out_specs=out_specs,
        out_shape=out_shape,
        compiler_params=_cparams(("parallel",), 32),
    )(proj2, proj2, nw.reshape(1, A_WIDTH), ws, bs_b)
    return res if emit_v else (res[0], None)


def _ret_kernel(q_ref, k_ref, v_ref, g_ref, cos_ref, sin_ref, dm_ref, qd_ref, kd_ref, sd_ref, nw_ref, *rest,
                has_init):
    if has_init:
        s0_ref, y_ref, sfin_ref, s_scr = rest
    else:
        y_ref, sfin_ref, s_scr = rest
    c = pl.program_id(2)

    @pl.when(c == 0)
    def _():
        s_scr[...] = s0_ref[...] if has_init else jnp.zeros_like(s_scr)

    cos = cos_ref[...]
    sin = sin_ref[...]
    q = q_ref[...].astype(F32)
    k = k_ref[...].astype(F32)
    qr = q * cos + pltpu.roll(q, B_DK // 2, 1) * sin
    kr = (k * cos + pltpu.roll(k, B_DK // 2, 1) * sin) * (B_DK ** -0.5)
    qb = qr.astype(BF16)
    kb = kr.astype(BF16)
    v = v_ref[...]
    attn = lax.dot_general(qb, kb, (((1,), (1,)), ((), ())), preferred_element_type=F32) * dm_ref[...]
    intra = jnp.dot(attn.astype(BF16), v, preferred_element_type=F32)
    s_prev = s_scr[...]
    inter = jnp.dot(qb, s_prev.astype(BF16), preferred_element_type=F32) * qd_ref[...]
    o = intra + inter
    kt = jnp.transpose(kr * kd_ref[...]).astype(BF16)
    s_new = sd_ref[...] * s_prev + jnp.dot(kt, v, preferred_element_type=F32)
    s_scr[...] = s_new
    gate = g_ref[...].astype(F32)
    y_ref[...] = (_rms(o, nw_ref[...]) * (gate * jax.nn.sigmoid(gate))).astype(y_ref.dtype)

    @pl.when(c == pl.num_programs(2) - 1)
    def _():
        sfin_ref[...] = s_new


def _ret_tables(c):
    lg = jnp.log1p(-jnp.exp2(-5.0 - jnp.arange(B_HEADS, dtype=F32)))
    idx = jnp.arange(c, dtype=F32)
    diff = idx[:, None] - idx[None, :]
    dm = jnp.where(diff >= 0, jnp.exp(jnp.maximum(diff, 0.0)[None] * lg[:, None, None]), 0.0)
    qd = jnp.exp((idx + 1.0)[None, :] * lg[:, None])
    kd = jnp.exp((c - 1.0 - idx)[None, :] * lg[:, None])
    sd = jnp.exp(c * lg)
    return (dm, jnp.broadcast_to(qd[:, :, None], (B_HEADS, c, B_DV)),
            jnp.broadcast_to(kd[:, :, None], (B_HEADS, c, B_DK)),
            jnp.broadcast_to(sd[:, None, None], (B_HEADS, 1, B_DV)))


def _rope_tables(pos):
    half = B_DK // 2
    inv_freq = 1.0 / (ROPE_BASE ** jnp.linspace(0.0, 1.0, half, dtype=F32))
    ang = pos.astype(F32)[:, None] * inv_freq[None, :]
    cos, sin = jnp.cos(ang), jnp.sin(ang)
    return jnp.concatenate([cos, cos], axis=-1), jnp.concatenate([-sin, sin], axis=-1)


def _retention(proj3, pos, nw, s0):
    bsz, seq, _ = proj3.shape
    c = _pick(seq, 256)
    dm, qd, kd, sd = _ret_tables(c)
    cos, sin = _rope_tables(pos)
    has_init = s0 is not None
    in_specs = [pl.BlockSpec((None, c, B_DK), lambda b, h, i: (b, i, OFF_BQ // B_DK + h)),
                pl.BlockSpec((None, c, B_DK), lambda b, h, i: (b, i, OFF_BK // B_DK + h)),
                pl.BlockSpec((None, c, B_DV), lambda b, h, i: (b, i, OFF_BV // B_DV + h)),
                pl.BlockSpec((None, c, B_DV), lambda b, h, i: (b, i, OFF_BG // B_DV + h)),
                pl.BlockSpec((c, B_DK), lambda b, h, i: (i, 0)),
                pl.BlockSpec((c, B_DK), lambda b, h, i: (i, 0)),
                pl.BlockSpec((None, c, c), lambda b, h, i: (h, 0, 0)),
                pl.BlockSpec((None, c, B_DV), lambda b, h, i: (h, 0, 0)),
                pl.BlockSpec((None, c, B_DK), lambda b, h, i: (h, 0, 0)),
                pl.BlockSpec((None, 1, B_DV), lambda b, h, i: (h, 0, 0)),
                pl.BlockSpec((1, B_DV), lambda b, h, i: (0, 0))]
    args = [proj3, proj3, proj3, proj3, cos, sin, dm, qd, kd, sd, nw.reshape(1, B_DV)]
    if has_init:
        in_specs.append(pl.BlockSpec((None, None, B_DK, B_DV), lambda b, h, i: (b, h, 0, 0)))
        args.append(s0)
    return pl.pallas_call(
        functools.partial(_ret_kernel, has_init=has_init),
        grid=(bsz, B_HEADS, seq // c),
        in_specs=in_specs,
        out_specs=[pl.BlockSpec((None, c, B_DV), lambda b, h, i: (b, i, h)),
                   pl.BlockSpec((None, None, B_DK, B_DV), lambda b, h, i: (b, h, 0, 0))],
        out_shape=[jax.ShapeDtypeStruct((bsz, seq, B_HEADS * B_DV), BF16),
                   jax.ShapeDtypeStruct((bsz, B_HEADS, B_DK, B_DV), F32)],
        scratch_shapes=[pltpu.VMEM((B_DK, B_DV), F32)],
        compiler_params=_cparams(("parallel", "parallel", "arbitrary"), 32),
    )(*args)


def _group_mean_matrix():
    r = lax.broadcasted_iota(jnp.int32, (LANES, LANES), 0)
    c = lax.broadcasted_iota(jnp.int32, (LANES, LANES), 1)
    return jnp.where((r // C_DH) == (c // C_DH), 1.0 / C_DH, 0.0).astype(BF16)


def _cprep_kernel(q_ref, k_ref, v_ref, qw_ref, kw_ref, lam_ref, qn_ref, kn_ref, kf_ref, vf_ref, lamo_ref, *rest,
                  lam_init, emit_vt):
    gm = _group_mean_matrix()
    qw = qw_ref[...]
    kw = kw_ref[...]
    for h in range(C_HEADS):
        cs = slice(h * LANES, (h + 1) * LANES)
        x = q_ref[:, cs].astype(F32)
        ms = jnp.dot((x * x).astype(BF16), gm, preferred_element_type=F32)
        qn_ref[:, cs] = (x * lax.rsqrt(ms + EPS) * qw * (C_DH ** -0.5)).astype(BF16)
        x = k_ref[:, cs].astype(F32)
        ms = jnp.dot((x * x).astype(BF16), gm, preferred_element_type=F32)
        kn = x * lax.rsqrt(ms + EPS) * kw
        kf_ref[:, cs] = kn
        kn_ref[:, cs] = kn.astype(BF16)
        vv = v_ref[:, cs].astype(F32)
        vf_ref[:, cs] = vv
        if emit_vt:
            rest[0][h] = jnp.transpose(vv).astype(BF16)
    lq = lam_ref[...]
    l01 = jnp.sum(lq[0:1] * lq[1:2], axis=-1, keepdims=True)
    l23 = jnp.sum(lq[2:3] * lq[3:4], axis=-1, keepdims=True)
    lam = jnp.exp(l01) - jnp.exp(l23) + lam_init
    lamo_ref[...] = jnp.broadcast_to(lam, lamo_ref.shape)


def _cprep(proj3, qw, kw, lam_p, lam_init, emit_vt):
    bsz, seq, _ = proj3.shape
    r = _pick(seq, 512)
    w = C_HEADS * LANES
    qw2 = jnp.concatenate([qw, qw]).reshape(1, LANES)
    kw2 = jnp.concatenate([kw, kw]).reshape(1, LANES)
    blk = lambda off: pl.BlockSpec((None, r, w), lambda b, i: (b, i, off // w))
    row = pl.BlockSpec((None, r, w), lambda b, i: (b, i, 0))
    out_shape = [jax.ShapeDtypeStruct((bsz, seq, w), BF16), jax.ShapeDtypeStruct((bsz, seq, w), BF16),
                 jax.ShapeDtypeStruct((bsz, seq, w), F32), jax.ShapeDtypeStruct((bsz, seq, w), F32),
                 jax.ShapeDtypeStruct((8, LANES), F32)]
    out_specs = [row, row, row, row, pl.BlockSpec((8, LANES), lambda b, i: (0, 0))]
    if emit_vt:
        out_shape.append(jax.ShapeDtypeStruct((bsz, C_HEADS, C_DV, seq), BF16))
        out_specs.append(pl.BlockSpec((None, C_HEADS, C_DV, r), lambda b, i: (b, 0, 0, i)))
    return pl.pallas_call(
        functools.partial(_cprep_kernel, lam_init=lam_init, emit_vt=emit_vt),
        grid=(bsz, seq // r),
        in_specs=[blk(OFF_CQ), blk(OFF_CK), blk(OFF_CV),
                  pl.BlockSpec((1, LANES), lambda b, i: (0, 0)),
                  pl.BlockSpec((1, LANES), lambda b, i: (0, 0)),
                  pl.BlockSpec((4, C_DH), lambda b, i: (0, 0))],
        out_specs=out_specs,
        out_shape=out_shape,
        compiler_params=_cparams(("arbitrary", "arbitrary"), 40),
    )(proj3, proj3, proj3, qw2, kw2, lam_p)


def _dattn_kernel(q_ref, k_ref, vt_ref, lam_ref, sw_ref, o_ref, m_scr, l_scr, a_scr, *, t, out_scale):
    qi = pl.program_id(2)
    q = q_ref[...]
    lane = lax.broadcasted_iota(jnp.int32, q.shape, 1)
    qc = (jnp.where(lane < C_DH, q, jnp.zeros_like(q)), jnp.where(lane >= C_DH, q, jnp.zeros_like(q)))
    m_scr[...] = jnp.full(m_scr.shape, NEG_BIG, F32)
    l_scr[...] = jnp.zeros(l_scr.shape, F32)
    a_scr[...] = jnp.zeros(a_scr.shape, F32)

    def block(j, masked):
        start = pl.multiple_of(j * t, t)
        kb = k_ref[pl.ds(start, t), :]
        vtb = vt_ref[:, pl.ds(start, t)]
        if masked:
            kpos = lax.broadcasted_iota(jnp.int32, (t, t), 0)
            qpos = lax.broadcasted_iota(jnp.int32, (t, t), 1)
            vis = (kpos // CHUNK) <= (qpos // CHUNK)
        for c in range(2):
            s = lax.dot_general(kb, qc[c], (((1,), (1,)), ((), ())), preferred_element_type=F32)
            if masked:
                s = jnp.where(vis, s, NEG_BIG)
            m_old = m_scr[c]
            m_new = jnp.maximum(m_old, jnp.max(s, axis=0, keepdims=True))
            alpha = jnp.exp(m_old - m_new)
            p = jnp.exp(s - m_new)
            l_scr[c] = alpha * l_scr[c] + jnp.sum(p, axis=0, keepdims=True)
            a_scr[c] = alpha * a_scr[c] + jnp.dot(vtb, p.astype(BF16), preferred_element_type=F32)
            m_scr[c] = m_new

    def body(j, carry):
        block(j, False)
        return carry

    lax.fori_loop(0, qi, body, 0)
    block(qi, True)

    lam = lam_ref[0:1, 0:1]
    o_t = a_scr[0] / l_scr[0] - lam * (a_scr[1] / l_scr[1])
    ms = jnp.mean(o_t * o_t, axis=0, keepdims=True)
    o = jnp.transpose(o_t * lax.rsqrt(ms + EPS))
    o_ref[...] = (o * sw_ref[...] * out_scale).astype(o_ref.dtype)


def _dattn_prompt(qn, kn, vt, lam_b, sw, out_scale):
    bsz, seq, w = qn.shape
    t = _pick(seq, 512)
    return pl.pallas_call(
        functools.partial(_dattn_kernel, t=t, out_scale=out_scale),
        grid=(bsz, C_HEADS, seq // t),
        in_specs=[pl.BlockSpec((None, t, LANES), lambda b, h, i: (b, i, h)),
                  pl.BlockSpec((None, seq, LANES), lambda b, h, i: (b, 0, h)),
                  pl.BlockSpec((None, None, C_DV, seq), lambda b, h, i: (b, h, 0, 0)),
                  pl.BlockSpec((8, LANES), lambda b, h, i: (0, 0)),
                  pl.BlockSpec((1, C_DV), lambda b, h, i: (0, 0))],
        out_specs=pl.BlockSpec((None, t, C_DV), lambda b, h, i: (b, i, h)),
        out_shape=jax.ShapeDtypeStruct((bsz, seq, w), BF16),
        scratch_shapes=[pltpu.VMEM((2, 1, t), F32), pltpu.VMEM((2, 1, t), F32), pltpu.VMEM((2, C_DV, t), F32)],
        compiler_params=_cparams(("parallel", "parallel", "arbitrary"), 40),
    )(qn, kn, vt, lam_b, sw.reshape(1, C_DV))


def _dattn_sample_kernel(q_ref, kc_ref, vc_ref, kn_ref, vn_ref, lam_ref, sw_ref, o_ref, *, past, out_scale):
    q = q_ref[...]
    lq = q.shape[0]
    lane = lax.broadcasted_iota(jnp.int32, q.shape, 1)
    kc = kc_ref[...].astype(BF16)
    vc = vc_ref[...].astype(BF16)
    kn = kn_ref[...]
    vn = vn_ref[...]
    qpos_c = past + lax.broadcasted_iota(jnp.int32, (lq, past), 0)
    kpos_c = lax.broadcasted_iota(jnp.int32, (lq, past), 1)
    vis_c = (kpos_c // CHUNK) <= (qpos_c // CHUNK)
    qpos_n = past + lax.broadcasted_iota(jnp.int32, (lq, lq), 0)
    kpos_n = past + lax.broadcasted_iota(jnp.int32, (lq, lq), 1)
    vis_n = (kpos_n // CHUNK) <= (qpos_n // CHUNK)
    outs = []
    for c in range(2):
        qc = jnp.where((lane // C_DH) == c, q, jnp.zeros_like(q))
        s_c = jnp.where(vis_c, lax.dot_general(qc, kc, (((1,), (1,)), ((), ())), preferred_element_type=F32), NEG_BIG)
        s_n = jnp.where(vis_n, lax.dot_general(qc, kn, (((1,), (1,)), ((), ())), preferred_element_type=F32), NEG_BIG)
        m = jnp.maximum(jnp.max(s_c, axis=-1, keepdims=True), jnp.max(s_n, axis=-1, keepdims=True))
        p_c = jnp.exp(s_c - m)
        p_n = jnp.exp(s_n - m)
        l = jnp.sum(p_c, axis=-1, keepdims=True) + jnp.sum(p_n, axis=-1, keepdims=True)
        acc = (jnp.dot(p_c.astype(BF16), vc, preferred_element_type=F32)
               + jnp.dot(p_n.astype(BF16), vn, preferred_element_type=F32))
        outs.append(acc / l)
    o = outs[0] - lam_ref[0:1, 0:1] * outs[1]
    o_ref[...] = (_rms(o, sw_ref[...]) * out_scale).astype(o_ref.dtype)


def _dattn_sample(qn, kn, proj3, cache_k, cache_v, lam_b, sw, out_scale):
    bsz, lq, w = qn.shape
    past = cache_k.shape[1]
    ck = cache_k.reshape(bsz, past, w)
    cv = cache_v.reshape(bsz, past, w)
    head = lambda rows: pl.BlockSpec((None, rows, LANES), lambda b, h: (b, 0, h))
    return pl.pallas_call(
        functools.partial(_dattn_sample_kernel, past=past, out_scale=out_scale),
        grid=(bsz, C_HEADS),
        in_specs=[head(lq), head(past), head(past), head(lq),
                  pl.BlockSpec((None, lq, LANES), lambda b, h: (b, 0, OFF_CV // LANES + h)),
                  pl.BlockSpec((8, LANES), lambda b, h: (0, 0)),
                  pl.BlockSpec((1, C_DV), lambda b, h: (0, 0))],
        out_specs=head(lq),
        out_shape=jax.ShapeDtypeStruct((bsz, lq, w), BF16),
        compiler_params=_cparams(("parallel", "parallel"), 32),
    )(qn, ck, cv, kn, proj3, lam_b, sw.reshape(1, C_DV))


def _merge_kernel(ya_ref, yb_ref, yc_ref, ga_ref, gb_ref, gc_ref, w_ref, o_ref):
    acc = None
    for y_ref, g_ref, n in ((ya_ref, ga_ref, 0), (yb_ref, gb_ref, 1), (yc_ref, gc_ref, 2)):
        up = jnp.dot(y_ref[...], w_ref[n], preferred_element_type=F32)
        term = jax.nn.sigmoid(g_ref[...].astype(F32)) * up
        acc = term if acc is None else acc + term
    o_ref[...] = acc.astype(o_ref.dtype)


def _merge(ya, yb, yc, proj2, wb_bf):
    n = ya.shape[0]
    tm = _pick(n, 512)
    tn = 1024
    ybs = pl.BlockSpec((tm, A_WIDTH), lambda j, i: (i, 0))
    gate = lambda b: pl.BlockSpec((tm, tn), lambda j, i: (i, (OFF_GATE + b * D_MODEL) // tn + j))
    return pl.pallas_call(
        _merge_kernel,
        grid=(D_MODEL // tn, n // tm),
        in_specs=[ybs, ybs, ybs, gate(0), gate(1), gate(2),
                  pl.BlockSpec((3, A_WIDTH, tn), lambda j, i: (0, 0, j))],
        out_specs=pl.BlockSpec((tm, tn), lambda j, i: (i, j)),
        out_shape=jax.ShapeDtypeStruct((n, D_MODEL), BF16),
        compiler_params=_cparams(("parallel", "parallel"), 48),
    )(ya, yb, yc, proj2, proj2, proj2, wb_bf)


def _outproj_kernel(u_ref, w_ref, x_ref, o_ref):
    o_ref[...] = x_ref[...] + jnp.dot(u_ref[...], w_ref[...], preferred_element_type=F32)


def _outproj(u, w_bf, x2):
    n = u.shape[0]
    tm = _pick(n, 512)
    return pl.pallas_call(
        _outproj_kernel,
        grid=(n // tm,),
        in_specs=[pl.BlockSpec((tm, D_MODEL), lambda i: (i, 0)),
                  pl.BlockSpec((D_MODEL, D_MODEL), lambda i: (0, 0)),
                  pl.BlockSpec((tm, D_MODEL), lambda i: (i, 0))],
        out_specs=pl.BlockSpec((tm, D_MODEL), lambda i: (i, 0)),
        out_shape=jax.ShapeDtypeStruct((n, D_MODEL), F32),
        compiler_params=_cparams(("parallel",), 48),
    )(u, w_bf, x2)


ROUTE_GROUP_LANE = N_EXPERTS


def _router_kernel(x_ref, nw_ref, whi_ref, wlo_ref, br_ref, o_ref):
    xn = _rms(x_ref[...], nw_ref[...])
    hi = xn.astype(BF16)
    lo = (xn - hi.astype(F32)).astype(BF16)
    logit = (jnp.dot(hi, whi_ref[...], preferred_element_type=F32)
             + jnp.dot(lo, whi_ref[...], preferred_element_type=F32)
             + jnp.dot(hi, wlo_ref[...], preferred_element_type=F32) + br_ref[...])
    lane = lax.broadcasted_iota(jnp.int32, logit.shape, 1).astype(F32)
    big = 1e9
    is_g = (lane >= ROUTE_GROUP_LANE) & (lane < ROUTE_GROUP_LANE + N_GROUPS)
    lg = jnp.where(is_g, logit, NEG_BIG)
    mg = jnp.max(lg, axis=-1, keepdims=True)
    g_sel = jnp.min(jnp.where(lg == mg, lane - ROUTE_GROUP_LANE, big), axis=-1, keepdims=True)
    p_group = 1.0 / jnp.sum(jnp.exp(lg - mg), axis=-1, keepdims=True)
    lo_lane = g_sel * EXPERTS_PER_GROUP
    in_g = (lane >= lo_lane) & (lane < lo_lane + EXPERTS_PER_GROUP)
    le = jnp.where(in_g, logit, NEG_BIG)
    v1 = jnp.max(le, axis=-1, keepdims=True)
    i1 = jnp.min(jnp.where(le == v1, lane, big), axis=-1, keepdims=True)
    le2 = jnp.where(lane == i1, NEG_BIG, le)
    v2 = jnp.max(le2, axis=-1, keepdims=True)
    i2 = jnp.min(jnp.where(le2 == v2, lane, big), axis=-1, keepdims=True)
    e2 = jnp.exp(v2 - v1)
    den = 1.0 + e2
    p1 = p_group / den
    p2 = p_group * e2 / den
    o_ref[...] = jnp.where(lane == 0, i1, jnp.where(lane == 1, i2, jnp.where(lane == 2, p1,
                           jnp.where(lane == 3, p2, 0.0))))


def _router(x2, nw, w_rg, b_rg, w_re, b_re):
    n = x2.shape[0]
    tm = _pick(n, 512)
    w = jnp.zeros((D_MODEL, LANES), F32).at[:, :N_EXPERTS].set(w_re).at[:, N_EXPERTS:N_EXPERTS + N_GROUPS].set(w_rg)
    br = jnp.zeros((1, LANES), F32).at[0, :N_EXPERTS].set(b_re).at[0, N_EXPERTS:N_EXPERTS + N_GROUPS].set(b_rg)
    whi = w.astype(BF16)
    wlo = (w - whi.astype(F32)).astype(BF16)
    full = lambda shape: pl.BlockSpec(shape, lambda i: (0, 0))
    return pl.pallas_call(
        _router_kernel,
        grid=(n // tm,),
        in_specs=[pl.BlockSpec((tm, D_MODEL), lambda i: (i, 0)), full((1, D_MODEL)),
                  full((D_MODEL, LANES)), full((D_MODEL, LANES)), full((1, LANES))],
        out_specs=pl.BlockSpec((tm, LANES), lambda i: (i, 0)),
        out_shape=jax.ShapeDtypeStruct((n, LANES), F32),
        compiler_params=_cparams(("parallel",), 32),
    )(x2, nw.reshape(1, D_MODEL), whi, wlo, br)


def _moe_plan(e1, e2, tm_e):
    n = e1.shape[0]
    eid = jnp.concatenate([e1, e2])
    ar = jnp.arange(N_EXPERTS, dtype=jnp.int32)
    oh = (eid[:, None] == ar[None, :]).astype(jnp.int32)
    cs = jnp.cumsum(oh, axis=0)
    cnt = cs[-1]
    offs = jnp.cumsum(cnt) - cnt
    dest = jnp.sum(oh * (offs[None, :] + cs - 1), axis=1)
    n_tiles = (2 * n) // tm_e
    first = offs // tm_e
    last = jnp.where(cnt > 0, (offs + cnt - 1) // tm_e, first - 1)
    n_items = last - first + 1
    item_end = jnp.cumsum(n_items)
    item_start = item_end - n_items
    n_work = n_tiles + N_EXPERTS - 1
    w = jnp.arange(n_work, dtype=jnp.int32)
    e_w = jnp.minimum(jnp.sum((w[:, None] >= item_end[None, :]).astype(jnp.int32), axis=1), N_EXPERTS - 1)
    valid = w < item_end[-1]
    ohw = (e_w[:, None] == ar[None, :]).astype(jnp.int32)
    sel = lambda tab: jnp.sum(ohw * tab[None, :], axis=1)
    tile = sel(first) + (w - sel(item_start))
    lo = jnp.maximum(sel(offs), tile * tm_e)
    hi = jnp.minimum(sel(offs) + sel(cnt), (tile + 1) * tm_e)
    last_e = jnp.max(jnp.where(n_items > 0, ar, 0))
    tile = jnp.where(valid, tile, n_tiles - 1)
    e_w = jnp.where(valid, e_w, last_e)
    lo = jnp.where(valid, lo, 0)
    hi = jnp.where(valid, hi, 0)
    return dest.astype(jnp.int32), tile.astype(jnp.int32), e_w.astype(jnp.int32), lo.astype(jnp.int32), hi.astype(jnp.int32)


def _row_copy(src_ref, src_row, dst_ref, dst_row, sem):
    return pltpu.make_async_copy(src_ref.at[pl.ds(src_row, 1), :], dst_ref.at[pl.ds(dst_row, 1), :], sem)


def _dispatch_kernel(dest_ref, x_ref, xs_ref, sem, *, tm):
    def issue(r, carry):
        _row_copy(x_ref, r, xs_ref, dest_ref[0, r], sem).start()
        _row_copy(x_ref, r, xs_ref, dest_ref[1, r], sem).start()
        return carry

    lax.fori_loop(0, tm, issue, 0)

    def drain(r, carry):
        _row_copy(x_ref, r, xs_ref, dest_ref[0, r], sem).wait()
        _row_copy(x_ref, r, xs_ref, dest_ref[1, r], sem).wait()
        return carry

    lax.fori_loop(0, tm, drain, 0)


def _dispatch(x2, dest3, tm):
    n = x2.shape[0]
    return pl.pallas_call(
        functools.partial(_dispatch_kernel, tm=tm),
        grid=(n // tm,),
        in_specs=[pl.BlockSpec((None, 2, tm), lambda i: (i, 0, 0), memory_space=pltpu.SMEM),
                  pl.BlockSpec((tm, D_MODEL), lambda i: (i, 0))],
        out_specs=pl.BlockSpec(memory_space=pl.ANY),
        out_shape=jax.ShapeDtypeStruct((2 * n, D_MODEL), F32),
        scratch_shapes=[pltpu.SemaphoreType.DMA(())],
        compiler_params=_cparams(("arbitrary",), 32),
    )(dest3, x2)


def _experts_kernel(tile_ref, exp_ref, lo_ref, hi_ref, xs_ref, nw_ref, wg_ref, wu_ref, wd_ref, ys_ref, *, tm):
    w = pl.program_id(0)
    lo = lo_ref[w]
    hi = hi_ref[w]
    tile = tile_ref[w]
    first_visit = jnp.logical_or(w == 0, tile_ref[jnp.maximum(w - 1, 0)] != tile)

    @pl.when(hi > lo)
    def _():
        xn = _rms(xs_ref[...], nw_ref[...]).astype(BF16)
        gate = jnp.dot(xn, wg_ref[...], preferred_element_type=F32)
        up = jnp.dot(xn, wu_ref[...], preferred_element_type=F32)
        h = (gate * jax.nn.sigmoid(gate) * up).astype(BF16)
        y = jnp.dot(h, wd_ref[...], preferred_element_type=F32)

        @pl.when(first_visit)
        def _():
            ys_ref[...] = y

        @pl.when(jnp.logical_not(first_visit))
        def _():
            rows = tile * tm + lax.broadcasted_iota(jnp.int32, (tm, 1), 0)
            ys_ref[...] = jnp.where((rows >= lo) & (rows < hi), y, ys_ref[...])


def _experts(xs, nw, wg_bf, wu_bf, wd_bf, tile, exp, lo, hi, tm):
    n_work = tile.shape[0]
    grid_spec = pltpu.PrefetchScalarGridSpec(
        num_scalar_prefetch=4,
        grid=(n_work,),
        in_specs=[pl.BlockSpec((tm, D_MODEL), lambda w, t, e, lo, hi: (t[w], 0)),
                  pl.BlockSpec((1, D_MODEL), lambda w, t, e, lo, hi: (0, 0)),
                  pl.BlockSpec((None, D_MODEL, D_EXPERT), lambda w, t, e, lo, hi: (e[w], 0, 0)),
                  pl.BlockSpec((None, D_MODEL, D_EXPERT), lambda w, t, e, lo, hi: (e[w], 0, 0)),
                  pl.BlockSpec((None, D_EXPERT, D_MODEL), lambda w, t, e, lo, hi: (e[w], 0, 0))],
        out_specs=pl.BlockSpec((tm, D_MODEL), lambda w, t, e, lo, hi: (t[w], 0)),
    )
    return pl.pallas_call(
        functools.partial(_experts_kernel, tm=tm),
        grid_spec=grid_spec,
        out_shape=jax.ShapeDtypeStruct(xs.shape, F32),
        compiler_params=_cparams(("arbitrary",), 48),
    )(tile, exp, lo, hi, xs, nw.reshape(1, D_MODEL), wg_bf, wu_bf, wd_bf)


def _combine_kernel(dest_ref, route_ref, x_ref, ys_ref, o_ref, buf0, buf1, sem, *, tm):
    def issue(r, carry):
        _row_copy(ys_ref, dest_ref[0, r], buf0, r, sem).start()
        _row_copy(ys_ref, dest_ref[1, r], buf1, r, sem).start()
        return carry

    lax.fori_loop(0, tm, issue, 0)

    def drain(r, carry):
        _row_copy(ys_ref, dest_ref[0, r], buf0, r, sem).wait()
        _row_copy(ys_ref, dest_ref[1, r], buf1, r, sem).wait()
        return carry

    lax.fori_loop(0, tm, drain, 0)
    route = route_ref[...]
    o_ref[...] = x_ref[...] + route[:, 2:3] * buf0[...] + route[:, 3:4] * buf1[...]


def _combine(x2, route, ys, dest3, tm):
    n = x2.shape[0]
    return pl.pallas_call(
        functools.partial(_combine_kernel, tm=tm),
        grid=(n // tm,),
        in_specs=[pl.BlockSpec((None, 2, tm), lambda i: (i, 0, 0), memory_space=pltpu.SMEM),
                  pl.BlockSpec((tm, LANES), lambda i: (i, 0)),
                  pl.BlockSpec((tm, D_MODEL), lambda i: (i, 0)),
                  pl.BlockSpec(memory_space=pl.ANY)],
        out_specs=pl.BlockSpec((tm, D_MODEL), lambda i: (i, 0)),
        out_shape=jax.ShapeDtypeStruct((n, D_MODEL), F32),
        scratch_shapes=[pltpu.VMEM((tm, D_MODEL), F32), pltpu.VMEM((tm, D_MODEL), F32),
                        pltpu.SemaphoreType.DMA(())],
        compiler_params=_cparams(("arbitrary",), 32),
    )(dest3, route, x2, ys)


def _ffn(x2, lw):
    n = x2.shape[0]
    route = _router(x2, lw["norm_ffn_w"], lw["w_rg"], lw["b_rg"], lw["w_re"], lw["b_re"])
    e1 = route[:, 0].astype(jnp.int32)
    e2 = route[:, 1].astype(jnp.int32)
    tm_e = _pick(2 * n, 512) if n >= 4096 else 128
    tm_r = _pick(n, 256)
    dest, tile, exp, lo, hi = _moe_plan(e1, e2, tm_e)
    dest3 = dest.reshape(2, n // tm_r, tm_r).transpose(1, 0, 2)
    xs = _dispatch(x2, dest3, tm_r)
    ys = _experts(xs, lw["norm_ffn_w"], lw["wg"], lw["wu"], lw["wd"], tile, exp, lo, hi, tm_e)
    return _combine(x2, route, ys, dest3, tm_r)


def _layer(x3, pos, l, lw, ret_state, kv_cache):
    bsz, seq, _ = x3.shape
    n = bsz * seq
    sample = ret_state is not None
    lam_init = 0.8 - 0.6 * math.exp(-0.3 * l)
    x2 = x3.reshape(n, D_MODEL)
    proj2 = _inproj(x2, lw["norm_mix_w"], lw["w_in"])
    proj3 = proj2.reshape(bsz, seq, D_IN)

    a_y, a_v = _gmlp(proj2, seq, lw["a_norm_w"], lw["a_ws"], lw["a_bs_b"], emit_v=sample)
    b_y, s_new = _retention(proj3, pos, lw["b_norm_w"], ret_state)
    prep = _cprep(proj3, lw["c_qnorm_w"], lw["c_knorm_w"], lw["c_lambda"], lam_init, emit_vt=not sample)
    qn, kn, kf, vf, lam_b = prep[:5]
    if sample:
        c_y = _dattn_sample(qn, kn, proj3, kv_cache[0], kv_cache[1], lam_b, lw["c_subln_w"], 1.0 - lam_init)
    else:
        c_y = _dattn_prompt(qn, kn, prep[5], lam_b, lw["c_subln_w"], 1.0 - lam_init)

    u = _merge(a_y, b_y.reshape(n, A_WIDTH), c_y.reshape(n, A_WIDTH), proj2, lw["w_branch"])
    x2 = _outproj(u, lw["w_out"], x2)
    x2 = _ffn(x2, lw)
    new_k = kf.reshape(bsz, seq, C_HEADS, 2 * C_DH)
    new_v = vf.reshape(bsz, seq, C_HEADS, C_DV)
    return x2.reshape(bsz, seq, D_MODEL), new_k, new_v, s_new, a_v


def kernel(x_prompt, x_sample, cache_k_c, cache_v_c, state_ret, norm_mix_w, w_in, a_norm_w, a_ws, a_bs, b_norm_w, c_qnorm_w, c_knorm_w, c_lambda, c_subln_w, w_branch, w_out, norm_ffn_w, w_router_group, b_router_group, w_router_expert, b_router_expert, w_gate_e, w_up_e, w_down_e):
    depth = w_in.shape[0]
    past = cache_k_c.shape[2]
    pos_p = jnp.arange(x_prompt.shape[1])
    pos_s = past + jnp.arange(x_sample.shape[1])
    yp, ys = x_prompt, x_sample
    outs = [[] for _ in range(7)]
    for l in range(depth):
        lw = dict(
            norm_mix_w=norm_mix_w[l], w_in=w_in[l].astype(BF16), a_norm_w=a_norm_w[l], a_ws=a_ws[l],
            a_bs_b=jnp.broadcast_to(a_bs[l][:, :, None], (A_GROUPS, A_BLOCK, A_GROUP)),
            b_norm_w=b_norm_w[l], c_qnorm_w=c_qnorm_w[l], c_knorm_w=c_knorm_w[l], c_lambda=c_lambda[l],
            c_subln_w=c_subln_w[l], w_branch=w_branch[l].astype(BF16), w_out=w_out[l].astype(BF16),
            norm_ffn_w=norm_ffn_w[l], w_rg=w_router_group[l], b_rg=b_router_group[l],
            w_re=w_router_expert[l], b_re=b_router_expert[l],
            wg=w_gate_e[l].astype(BF16), wu=w_up_e[l].astype(BF16), wd=w_down_e[l].astype(BF16))
        yp, kp, vp, rp, _ = _layer(yp, pos_p, l, lw, None, None)
        ys, kn, vn, rn, avn = _layer(ys, pos_s, l, lw, state_ret[l], (cache_k_c[l], cache_v_c[l]))
        for lst, val in zip(outs, (kp, vp, rp, kn, vn, rn, avn.reshape(ys.shape[0], ys.shape[1], A_WIDTH))):
            lst.append(val)
    return (yp, ys) + tuple(jnp.stack(o, 0) for o in outs)
```

```python
import functools
import math

import jax
import jax.numpy as jnp
from jax import lax
from jax.experimental import pallas as pl
from jax.experimental.pallas import tpu as pltpu

F32 = jnp.float32
BF16 = jnp.bfloat16

D_MODEL = 2048
CHUNK = 64
A_WIDTH = 1024
A_BLOCK = 128
A_GROUP = 128
A_GROUPS = 8
B_HEADS = 4
B_DK = 128
B_DV = 256
ROPE_BASE = 10000.0
C_HEADS = 8
C_DH = 64
C_DV = 128
N_GROUPS = 4
EXPERTS_PER_GROUP = 4
N_EXPERTS = 16
D_EXPERT = 512
EPS = 1e-6
D_IN = 14336

OFF_AU, OFF_AV, OFF_BQ, OFF_BK, OFF_BV, OFF_BG, OFF_CQ, OFF_CK, OFF_CV, OFF_GATE = (
    0, 1024, 2048, 2560, 3072, 4096, 5120, 6144, 7168, 8192)

LANES = 128
MIB = 1024 * 1024
NEG_BIG = -1e30
Q_SCALE = (C_DH ** -0.5) * math.log2(math.e)


def _cparams(sem, vmem_mib):
    return pltpu.CompilerParams(dimension_semantics=sem, vmem_limit_bytes=vmem_mib * MIB)


def _rms(x, w):
    ms = jnp.mean(x * x, axis=-1, keepdims=True)
    return x * lax.rsqrt(ms + EPS) * w


def _pick(n, pref):
    t = min(pref, n)
    while n % t:
        t //= 2
    return t


def _inproj_kernel(x_ref, g_ref, w_ref, o_ref, xn_ref):
    @pl.when(pl.program_id(1) == 0)
    def _():
        xn_ref[...] = _rms(x_ref[...], g_ref[...]).astype(BF16)

    o_ref[...] = jnp.dot(xn_ref[...], w_ref[...], preferred_element_type=F32).astype(o_ref.dtype)


def _inproj(x2, g, w_bf):
    n = x2.shape[0]
    tm = _pick(n, 1024)
    tn = 1024
    return pl.pallas_call(
        _inproj_kernel,
        grid=(n // tm, D_IN // tn),
        in_specs=[pl.BlockSpec((tm, D_MODEL), lambda i, j: (i, 0)),
                  pl.BlockSpec((1, D_MODEL), lambda i, j: (0, 0)),
                  pl.BlockSpec((D_MODEL, tn), lambda i, j: (0, j))],
        out_specs=pl.BlockSpec((tm, tn), lambda i, j: (i, j)),
        out_shape=jax.ShapeDtypeStruct((n, D_IN), BF16),
        scratch_shapes=[pltpu.VMEM((tm, D_MODEL), BF16)],
        compiler_params=_cparams(("parallel", "arbitrary"), 48),
    )(x2, g.reshape(1, D_MODEL), w_bf)


def _gmlp_kernel(u_ref, v_ref, nw_ref, ws_ref, bs_ref, y_ref, *rest, pb, nblk, emit_v):
    u = jax.nn.gelu(u_ref[...].astype(F32))
    v = _rms(jax.nn.gelu(v_ref[...].astype(F32)), nw_ref[...])
    if emit_v:
        rest[0][...] = v
    vb = v.astype(BF16)
    p = lax.broadcasted_iota(jnp.int32, (A_BLOCK, A_BLOCK), 0)
    q = lax.broadcasted_iota(jnp.int32, (A_BLOCK, A_BLOCK), 1)
    mask = (q // CHUNK) <= (p // CHUNK)
    for g in range(A_GROUPS):
        w = jnp.where(mask, ws_ref[g], 0.0).astype(BF16)
        bias = bs_ref[g, :pb, :]
        cs = slice(g * A_GROUP, (g + 1) * A_GROUP)
        for b in range(nblk):
            rs = slice(b * pb, (b + 1) * pb)
            vg = vb[rs, cs]
            if pb < A_BLOCK:
                vg = jnp.concatenate([vg, jnp.zeros((A_BLOCK - pb, A_GROUP), BF16)], axis=0)
            s = jnp.dot(w, vg, preferred_element_type=F32)[:pb] + bias
            y_ref[rs, cs] = (u[rs, cs] * s).astype(y_ref.dtype)


def _gmlp(proj2, seq_len, nw, ws, bs_b, emit_v):
    n = proj2.shape[0]
    if seq_len % A_BLOCK == 0:
        pb, nblk = A_BLOCK, 4 if seq_len % (4 * A_BLOCK) == 0 else 1
    else:
        assert seq_len <= A_BLOCK
        pb, nblk = seq_len, 1
    r = pb * nblk
    out_shape = [jax.ShapeDtypeStruct((n, A_WIDTH), BF16)]
    out_specs = [pl.BlockSpec((r, A_WIDTH), lambda i: (i, 0))]
    if emit_v:
        out_shape.append(jax.ShapeDtypeStruct((n, A_WIDTH), F32))
        out_specs.append(pl.BlockSpec((r, A_WIDTH), lambda i: (i, 0)))
    res = pl.pallas_call(
        functools.partial(_gmlp_kernel, pb=pb, nblk=nblk, emit_v=emit_v),
        grid=(n // r,),
        in_specs=[pl.BlockSpec((r, A_WIDTH), lambda i: (i, OFF_AU // A_WIDTH)),
                  pl.BlockSpec((r, A_WIDTH), lambda i: (i, OFF_AV // A_WIDTH)),
                  pl.BlockSpec((1, A_WIDTH), lambda i: (0, 0)),
                  pl.BlockSpec((A_GROUPS, A_BLOCK, A_BLOCK), lambda i: (0, 0, 0)),
                  pl.BlockSpec((A_GROUPS, A_BLOCK, A_GROUP), lambda i: (0, 0, 0))],
        out_specs=out_specs,
        out_shape=out_shape,
        compiler_params=_cparams(("parallel",), 32),
    )(proj2, proj2, nw.reshape(1, A_WIDTH), ws, bs_b)
    return res if emit_v else (res[0], None)


def _ret_kernel(q_ref, k_ref, v_ref, g_ref, cos_ref, sin_ref, dm_ref, qd_ref, kd_ref, sd_ref, nw_ref, *rest,
                has_init):
    if has_init:
        s0_ref, y_ref, sfin_ref, s_scr = rest
    else:
        y_ref, sfin_ref, s_scr = rest
    c = pl.program_id(1)

    @pl.when(c == 0)
    def _():
        s_scr[...] = s0_ref[...] if has_init else jnp.zeros_like(s_scr)

    cos = cos_ref[...]
    sin = sin_ref[...]
    nw = nw_ref[...]
    for h in range(B_HEADS):
        ks = slice(h * B_DK, (h + 1) * B_DK)
        vs = slice(h * B_DV, (h + 1) * B_DV)
        q = q_ref[:, ks].astype(F32)
        k = k_ref[:, ks].astype(F32)
        qr = q * cos + pltpu.roll(q, B_DK // 2, 1) * sin
        kr = (k * cos + pltpu.roll(k, B_DK // 2, 1) * sin) * (B_DK ** -0.5)
        qb = qr.astype(BF16)
        kb = kr.astype(BF16)
        v = v_ref[:, vs]
        attn = lax.dot_general(qb, kb, (((1,), (1,)), ((), ())), preferred_element_type=F32) * dm_ref[h]
        intra = jnp.dot(attn.astype(BF16), v, preferred_element_type=F32)
        s_prev = s_scr[h]
        inter = jnp.dot(qb, s_prev.astype(BF16), preferred_element_type=F32) * qd_ref[h]
        o = intra + inter
        kt = jnp.transpose(kr * kd_ref[h]).astype(BF16)
        s_scr[h] = sd_ref[h] * s_prev + jnp.dot(kt, v, preferred_element_type=F32)
        gate = g_ref[:, vs].astype(F32)
        y_ref[:, vs] = (_rms(o, nw) * (gate * jax.nn.sigmoid(gate))).astype(y_ref.dtype)

    @pl.when(c == pl.num_programs(1) - 1)
    def _():
        sfin_ref[...] = s_scr[...]


def _ret_tables(c):
    lg = jnp.log1p(-jnp.exp2(-5.0 - jnp.arange(B_HEADS, dtype=F32)))
    idx = jnp.arange(c, dtype=F32)
    diff = idx[:, None] - idx[None, :]
    dm = jnp.where(diff >= 0, jnp.exp(jnp.maximum(diff, 0.0)[None] * lg[:, None, None]), 0.0)
    qd = jnp.exp((idx + 1.0)[None, :] * lg[:, None])
    kd = jnp.exp((c - 1.0 - idx)[None, :] * lg[:, None])
    sd = jnp.exp(c * lg)
    return (dm, jnp.broadcast_to(qd[:, :, None], (B_HEADS, c, B_DV)),
            jnp.broadcast_to(kd[:, :, None], (B_HEADS, c, B_DK)),
            jnp.broadcast_to(sd[:, None, None], (B_HEADS, 1, B_DV)))


def _rope_tables(pos):
    half = B_DK // 2
    inv_freq = 1.0 / (ROPE_BASE ** jnp.linspace(0.0, 1.0, half, dtype=F32))
    ang = pos.astype(F32)[:, None] * inv_freq[None, :]
    cos, sin = jnp.cos(ang), jnp.sin(ang)
    return jnp.concatenate([cos, cos], axis=-1), jnp.concatenate([-sin, sin], axis=-1)


def _retention(proj3, pos, nw, s0):
    bsz, seq, _ = proj3.shape
    c = _pick(seq, 256)
    dm, qd, kd, sd = _ret_tables(c)
    cos, sin = _rope_tables(pos)
    has_init = s0 is not None
    qk_w = B_HEADS * B_DK
    v_w = B_HEADS * B_DV
    full = lambda shape: pl.BlockSpec(shape, lambda b, i: (0,) * len(shape))
    state = pl.BlockSpec((None, B_HEADS, B_DK, B_DV), lambda b, i: (b, 0, 0, 0))
    in_specs = [pl.BlockSpec((None, c, qk_w), lambda b, i: (b, i, OFF_BQ // qk_w)),
                pl.BlockSpec((None, c, qk_w), lambda b, i: (b, i, OFF_BK // qk_w)),
                pl.BlockSpec((None, c, v_w), lambda b, i: (b, i, OFF_BV // v_w)),
                pl.BlockSpec((None, c, v_w), lambda b, i: (b, i, OFF_BG // v_w)),
                pl.BlockSpec((c, B_DK), lambda b, i: (i, 0)),
                pl.BlockSpec((c, B_DK), lambda b, i: (i, 0)),
                full((B_HEADS, c, c)), full((B_HEADS, c, B_DV)), full((B_HEADS, c, B_DK)),
                full((B_HEADS, 1, B_DV)), full((1, B_DV))]
    args = [proj3, proj3, proj3, proj3, cos, sin, dm, qd, kd, sd, nw.reshape(1, B_DV)]
    if has_init:
        in_specs.append(state)
        args.append(s0)
    return pl.pallas_call(
        functools.partial(_ret_kernel, has_init=has_init),
        grid=(bsz, seq // c),
        in_specs=in_specs,
        out_specs=[pl.BlockSpec((None, c, v_w), lambda b, i: (b, i, 0)), state],
        out_shape=[jax.ShapeDtypeStruct((bsz, seq, v_w), BF16),
                   jax.ShapeDtypeStruct((bsz, B_HEADS, B_DK, B_DV), F32)],
        scratch_shapes=[pltpu.VMEM((B_HEADS, B_DK, B_DV), F32)],
        compiler_params=_cparams(("parallel", "arbitrary"), 32),
    )(*args)


def _group_mean_matrix():
    r = lax.broadcasted_iota(jnp.int32, (LANES, LANES), 0)
    c = lax.broadcasted_iota(jnp.int32, (LANES, LANES), 1)
    return jnp.where((r // C_DH) == (c // C_DH), 1.0 / C_DH, 0.0).astype(BF16)


def _cprep_kernel(q_ref, k_ref, v_ref, qw_ref, kw_ref, lam_ref, qn_ref, kn_ref, kf_ref, vf_ref, lamo_ref, *rest,
                  lam_init, emit_vt):
    gm = _group_mean_matrix()
    qw = qw_ref[...]
    kw = kw_ref[...]
    for h in range(C_HEADS):
        cs = slice(h * LANES, (h + 1) * LANES)
        x = q_ref[:, cs].astype(F32)
        ms = jnp.dot((x * x).astype(BF16), gm, preferred_element_type=F32)
        qn_ref[:, cs] = (x * lax.rsqrt(ms + EPS) * qw * Q_SCALE).astype(BF16)
        x = k_ref[:, cs].astype(F32)
        ms = jnp.dot((x * x).astype(BF16), gm, preferred_element_type=F32)
        kn = x * lax.rsqrt(ms + EPS) * kw
        kf_ref[:, cs] = kn
        kn_ref[:, cs] = kn.astype(BF16)
        vv = v_ref[:, cs].astype(F32)
        vf_ref[:, cs] = vv
        if emit_vt:
            rest[0][h] = jnp.transpose(vv).astype(BF16)
    lq = lam_ref[...]
    l01 = jnp.sum(lq[0:1] * lq[1:2], axis=-1, keepdims=True)
    l23 = jnp.sum(lq[2:3] * lq[3:4], axis=-1, keepdims=True)
    lam = jnp.exp(l01) - jnp.exp(l23) + lam_init
    lamo_ref[...] = jnp.broadcast_to(lam, lamo_ref.shape)


def _cprep(proj3, qw, kw, lam_p, lam_init, emit_vt):
    bsz, seq, _ = proj3.shape
    r = _pick(seq, 512)
    w = C_HEADS * LANES
    qw2 = jnp.concatenate([qw, qw]).reshape(1, LANES)
    kw2 = jnp.concatenate([kw, kw]).reshape(1, LANES)
    blk = lambda off: pl.BlockSpec((None, r, w), lambda b, i: (b, i, off // w))
    row = pl.BlockSpec((None, r, w), lambda b, i: (b, i, 0))
    out_shape = [jax.ShapeDtypeStruct((bsz, seq, w), BF16), jax.ShapeDtypeStruct((bsz, seq, w), BF16),
                 jax.ShapeDtypeStruct((bsz, seq, w), F32), jax.ShapeDtypeStruct((bsz, seq, w), F32),
                 jax.ShapeDtypeStruct((8, LANES), F32)]
    out_specs = [row, row, row, row, pl.BlockSpec((8, LANES), lambda b, i: (0, 0))]
    if emit_vt:
        out_shape.append(jax.ShapeDtypeStruct((bsz, C_HEADS, C_DV, seq), BF16))
        out_specs.append(pl.BlockSpec((None, C_HEADS, C_DV, r), lambda b, i: (b, 0, 0, i)))
    return pl.pallas_call(
        functools.partial(_cprep_kernel, lam_init=lam_init, emit_vt=emit_vt),
        grid=(bsz, seq // r),
        in_specs=[blk(OFF_CQ), blk(OFF_CK), blk(OFF_CV),
                  pl.BlockSpec((1, LANES), lambda b, i: (0, 0)),
                  pl.BlockSpec((1, LANES), lambda b, i: (0, 0)),
                  pl.BlockSpec((4, C_DH), lambda b, i: (0, 0))],
        out_specs=out_specs,
        out_shape=out_shape,
        compiler_params=_cparams(("arbitrary", "arbitrary"), 40),
    )(proj3, proj3, proj3, qw2, kw2, lam_p)


def _dattn_kernel(q_ref, k_ref, vt_ref, lam_ref, sw_ref, o_ref, s_a, s_b, m_scr, l_scr, a_scr, *, t, out_scale):
    qi = pl.program_id(2)
    q = q_ref[...]
    lane = lax.broadcasted_iota(jnp.int32, q.shape, 1)
    qc = (jnp.where(lane < C_DH, q, jnp.zeros_like(q)), jnp.where(lane >= C_DH, q, jnp.zeros_like(q)))
    m_scr[...] = jnp.full(m_scr.shape, NEG_BIG, F32)
    l_scr[...] = jnp.zeros(l_scr.shape, F32)
    a_scr[...] = jnp.zeros(a_scr.shape, F32)

    def scores(blk, s_ref):
        kb = k_ref[pl.ds(pl.multiple_of(blk * t, t), t), :]
        for c in range(2):
            s_ref[c] = lax.dot_general(kb, qc[c], (((1,), (1,)), ((), ())), preferred_element_type=F32)

    def update(blk, s_ref, masked=False):
        vtb = vt_ref[:, pl.ds(pl.multiple_of(blk * t, t), t)]
        if masked:
            kpos = lax.broadcasted_iota(jnp.int32, (t, t), 0)
            qpos = lax.broadcasted_iota(jnp.int32, (t, t), 1)
            vis = (kpos // CHUNK) <= (qpos // CHUNK)
        for c in range(2):
            s = s_ref[c]
            if masked:
                s = jnp.where(vis, s, NEG_BIG)
            m_old = m_scr[c]
            m_new = jnp.maximum(m_old, jnp.max(s, axis=0, keepdims=True))
            alpha = jnp.exp2(m_old - m_new)
            p = jnp.exp2(s - m_new)
            l_scr[c] = alpha * l_scr[c] + jnp.sum(p, axis=0, keepdims=True)
            a_scr[c] = alpha * a_scr[c] + jnp.dot(vtb, p.astype(BF16), preferred_element_type=F32)
            m_scr[c] = m_new

    scores(qi, s_a)

    @pl.when(qi == 0)
    def _():
        update(qi, s_a, masked=True)

    @pl.when(qi > 0)
    def _():
        scores(0, s_b)
        update(qi, s_a, masked=True)

        def pair(jj, carry):
            j = 2 * jj
            scores(j + 1, s_a)
            update(j, s_b)
            scores(j + 2, s_b)
            update(j + 1, s_a)
            return carry

        lax.fori_loop(0, (qi - 1) // 2, pair, 0)

        @pl.when(qi % 2 == 1)
        def _():
            update(qi - 1, s_b)

        @pl.when(qi % 2 == 0)
        def _():
            scores(qi - 1, s_a)
            update(qi - 2, s_b)
            update(qi - 1, s_a)

    lam = lam_ref[0:1, 0:1]
    o_t = a_scr[0] / l_scr[0] - lam * (a_scr[1] / l_scr[1])
    ms = jnp.mean(o_t * o_t, axis=0, keepdims=True)
    o = jnp.transpose(o_t * lax.rsqrt(ms + EPS))
    o_ref[...] = (o * sw_ref[...] * out_scale).astype(o_ref.dtype)


def _dattn_prompt(qn, kn, vt, lam_b, sw, out_scale):
    bsz, seq, w = qn.shape
    t = _pick(seq, 512)
    return pl.pallas_call(
        functools.partial(_dattn_kernel, t=t, out_scale=out_scale),
        grid=(bsz, C_HEADS, seq // t),
        in_specs=[pl.BlockSpec((None, t, LANES), lambda b, h, i: (b, i, h)),
                  pl.BlockSpec((None, seq, LANES), lambda b, h, i: (b, 0, h)),
                  pl.BlockSpec((None, None, C_DV, seq), lambda b, h, i: (b, h, 0, 0)),
                  pl.BlockSpec((8, LANES), lambda b, h, i: (0, 0)),
                  pl.BlockSpec((1, C_DV), lambda b, h, i: (0, 0))],
        out_specs=pl.BlockSpec((None, t, C_DV), lambda b, h, i: (b, i, h)),
        out_shape=jax.ShapeDtypeStruct((bsz, seq, w), BF16),
        scratch_shapes=[pltpu.VMEM((2, t, t), F32), pltpu.VMEM((2, t, t), F32),
                        pltpu.VMEM((2, 1, t), F32), pltpu.VMEM((2, 1, t), F32), pltpu.VMEM((2, C_DV, t), F32)],
        compiler_params=_cparams(("parallel", "parallel", "arbitrary"), 40),
    )(qn, kn, vt, lam_b, sw.reshape(1, C_DV))


def _dattn_sample_kernel(q_ref, kc_ref, vc_ref, kn_ref, vn_ref, lam_ref, sw_ref, o_ref, *, past, out_scale):
    lq = q_ref.shape[0]
    lane = lax.broadcasted_iota(jnp.int32, (lq, LANES), 1)
    qpos_c = past + lax.broadcasted_iota(jnp.int32, (lq, past), 0)
    kpos_c = lax.broadcasted_iota(jnp.int32, (lq, past), 1)
    vis_c = (kpos_c // CHUNK) <= (qpos_c // CHUNK)
    qpos_n = past + lax.broadcasted_iota(jnp.int32, (lq, lq), 0)
    kpos_n = past + lax.broadcasted_iota(jnp.int32, (lq, lq), 1)
    vis_n = (kpos_n // CHUNK) <= (qpos_n // CHUNK)
    lam = lam_ref[0:1, 0:1]
    nt = (((1,), (1,)), ((), ()))
    for h in range(C_HEADS):
        cs = slice(h * LANES, (h + 1) * LANES)
        q = q_ref[:, cs]
        kc = kc_ref[:, cs].astype(BF16)
        vc = vc_ref[:, cs].astype(BF16)
        kn = kn_ref[:, cs]
        vn = vn_ref[:, cs]
        outs = []
        for c in range(2):
            qc = jnp.where((lane // C_DH) == c, q, jnp.zeros_like(q))
            s_c = jnp.where(vis_c, lax.dot_general(qc, kc, nt, preferred_element_type=F32), NEG_BIG)
            s_n = jnp.where(vis_n, lax.dot_general(qc, kn, nt, preferred_element_type=F32), NEG_BIG)
            m = jnp.maximum(jnp.max(s_c, axis=-1, keepdims=True), jnp.max(s_n, axis=-1, keepdims=True))
            p_c = jnp.exp2(s_c - m)
            p_n = jnp.exp2(s_n - m)
            l = jnp.sum(p_c, axis=-1, keepdims=True) + jnp.sum(p_n, axis=-1, keepdims=True)
            acc = (jnp.dot(p_c.astype(BF16), vc, preferred_element_type=F32)
                   + jnp.dot(p_n.astype(BF16), vn, preferred_element_type=F32))
            outs.append(acc / l)
        o = outs[0] - lam * outs[1]
        o_ref[:, cs] = (_rms(o, sw_ref[...]) * out_scale).astype(o_ref.dtype)


def _dattn_sample(qn, kn, proj3, cache_k, cache_v, lam_b, sw, out_scale):
    bsz, lq, w = qn.shape
    past = cache_k.shape[1]
    ck = cache_k.reshape(bsz, past, w)
    cv = cache_v.reshape(bsz, past, w)
    rows = lambda r: pl.BlockSpec((None, r, w), lambda b: (b, 0, 0))
    return pl.pallas_call(
        functools.partial(_dattn_sample_kernel, past=past, out_scale=out_scale),
        grid=(bsz,),
        in_specs=[rows(lq), rows(past), rows(past), rows(lq),
                  pl.BlockSpec((None, lq, w), lambda b: (b, 0, OFF_CV // w)),
                  pl.BlockSpec((8, LANES), lambda b: (0, 0)),
                  pl.BlockSpec((1, C_DV), lambda b: (0, 0))],
        out_specs=rows(lq),
        out_shape=jax.ShapeDtypeStruct((bsz, lq, w), BF16),
        compiler_params=_cparams(("parallel",), 40),
    )(qn, ck, cv, kn, proj3, lam_b, sw.reshape(1, C_DV))


def _merge_kernel(ya_ref, yb_ref, yc_ref, ga_ref, gb_ref, gc_ref, w_ref, o_ref):
    acc = None
    for y_ref, g_ref, n in ((ya_ref, ga_ref, 0), (yb_ref, gb_ref, 1), (yc_ref, gc_ref, 2)):
        up = jnp.dot(y_ref[...], w_ref[n], preferred_element_type=F32)
        term = jax.nn.sigmoid(g_ref[...].astype(F32)) * up
        acc = term if acc is None else acc + term
    o_ref[...] = acc.astype(o_ref.dtype)


def _merge(ya, yb, yc, proj2, wb_bf):
    n = ya.shape[0]
    tm = _pick(n, 512)
    tn = 1024
    ybs = pl.BlockSpec((tm, A_WIDTH), lambda j, i: (i, 0))
    gate = lambda b: pl.BlockSpec((tm, tn), lambda j, i: (i, (OFF_GATE + b * D_MODEL) // tn + j))
    return pl.pallas_call(
        _merge_kernel,
        grid=(D_MODEL // tn, n // tm),
        in_specs=[ybs, ybs, ybs, gate(0), gate(1), gate(2),
                  pl.BlockSpec((3, A_WIDTH, tn), lambda j, i: (0, 0, j))],
        out_specs=pl.BlockSpec((tm, tn), lambda j, i: (i, j)),
        out_shape=jax.ShapeDtypeStruct((n, D_MODEL), BF16),
        compiler_params=_cparams(("parallel", "parallel"), 48),
    )(ya, yb, yc, proj2, proj2, proj2, wb_bf)


def _outproj_kernel(u_ref, w_ref, x_ref, o_ref):
    o_ref[...] = x_ref[...] + jnp.dot(u_ref[...], w_ref[...], preferred_element_type=F32)


def _outproj(u, w_bf, x2):
    n = u.shape[0]
    tm = _pick(n, 512)
    return pl.pallas_call(
        _outproj_kernel,
        grid=(n // tm,),
        in_specs=[pl.BlockSpec((tm, D_MODEL), lambda i: (i, 0)),
                  pl.BlockSpec((D_MODEL, D_MODEL), lambda i: (0, 0)),
                  pl.BlockSpec((tm, D_MODEL), lambda i: (i, 0))],
        out_specs=pl.BlockSpec((tm, D_MODEL), lambda i: (i, 0)),
        out_shape=jax.ShapeDtypeStruct((n, D_MODEL), F32),
        compiler_params=_cparams(("parallel",), 48),
    )(u, w_bf, x2)


ROUTE_GROUP_LANE = N_EXPERTS


def _router_kernel(x_ref, nw_ref, wcat_ref, br_ref, o_ref, ot_ref, cnt_ref, run_scr):
    @pl.when(pl.program_id(0) == 0)
    def _():
        run_scr[...] = jnp.zeros_like(run_scr)

    tm = x_ref.shape[0]
    xn = _rms(x_ref[...], nw_ref[...])
    hi = xn.astype(BF16)
    lo = (xn - hi.astype(F32)).astype(BF16)
    both = jnp.dot(hi, wcat_ref[...], preferred_element_type=F32)
    logit = (both[:, :LANES] + both[:, LANES:]
             + jnp.dot(lo, wcat_ref[:, :LANES], preferred_element_type=F32) + br_ref[...])
    lane = lax.broadcasted_iota(jnp.int32, logit.shape, 1).astype(F32)
    big = 1e9
    is_g = (lane >= ROUTE_GROUP_LANE) & (lane < ROUTE_GROUP_LANE + N_GROUPS)
    lg = jnp.where(is_g, logit, NEG_BIG)
    mg = jnp.max(lg, axis=-1, keepdims=True)
    g_sel = jnp.min(jnp.where(lg == mg, lane - ROUTE_GROUP_LANE, big), axis=-1, keepdims=True)
    p_group = 1.0 / jnp.sum(jnp.exp(lg - mg), axis=-1, keepdims=True)
    lo_lane = g_sel * EXPERTS_PER_GROUP
    in_g = (lane >= lo_lane) & (lane < lo_lane + EXPERTS_PER_GROUP)
    le = jnp.where(in_g, logit, NEG_BIG)
    v1 = jnp.max(le, axis=-1, keepdims=True)
    i1 = jnp.min(jnp.where(le == v1, lane, big), axis=-1, keepdims=True)
    le2 = jnp.where(lane == i1, NEG_BIG, le)
    v2 = jnp.max(le2, axis=-1, keepdims=True)
    i2 = jnp.min(jnp.where(le2 == v2, lane, big), axis=-1, keepdims=True)
    e2 = jnp.exp(v2 - v1)
    den = 1.0 + e2
    p1 = p_group / den
    p2 = p_group * e2 / den
    oh1 = jnp.where(lane == i1, 1.0, 0.0)
    oh2 = jnp.where(lane == i2, 1.0, 0.0)
    oh = oh1 + oh2
    row = lax.broadcasted_iota(jnp.int32, (tm, tm), 0)
    col = lax.broadcasted_iota(jnp.int32, (tm, tm), 1)
    tri = jnp.where(row > col, 1.0, 0.0).astype(BF16)
    before = run_scr[...] + jnp.dot(tri, oh.astype(BF16), preferred_element_type=F32)
    r1 = jnp.sum(oh1 * before, axis=-1, keepdims=True)
    r2 = jnp.sum(oh2 * before, axis=-1, keepdims=True)
    run = run_scr[...] + jnp.sum(oh, axis=0, keepdims=True)
    run_scr[...] = run
    cnt_ref[...] = jnp.broadcast_to(run, cnt_ref.shape)
    out = jnp.zeros_like(logit)
    for n, val in enumerate((i1, i2, p1, p2, r1, r2)):
        out = jnp.where(lane == n, val, out)
    o_ref[...] = out
    ot_ref[...] = jnp.transpose(out)[:8]


def _router(x2, nw, w_rg, b_rg, w_re, b_re):
    n = x2.shape[0]
    tm = _pick(n, 512)
    w = jnp.zeros((D_MODEL, LANES), F32).at[:, :N_EXPERTS].set(w_re).at[:, N_EXPERTS:N_EXPERTS + N_GROUPS].set(w_rg)
    br = jnp.zeros((1, LANES), F32).at[0, :N_EXPERTS].set(b_re).at[0, N_EXPERTS:N_EXPERTS + N_GROUPS].set(b_rg)
    whi = w.astype(BF16)
    wlo = (w - whi.astype(F32)).astype(BF16)
    full = lambda shape: pl.BlockSpec(shape, lambda i: (0, 0))
    return pl.pallas_call(
        _router_kernel,
        grid=(n // tm,),
        in_specs=[pl.BlockSpec((tm, D_MODEL), lambda i: (i, 0)), full((1, D_MODEL)),
                  full((D_MODEL, 2 * LANES)), full((1, LANES))],
        out_specs=[pl.BlockSpec((tm, LANES), lambda i: (i, 0)), pl.BlockSpec((8, tm), lambda i: (0, i)),
                   full((8, LANES))],
        out_shape=[jax.ShapeDtypeStruct((n, LANES), F32), jax.ShapeDtypeStruct((8, n), F32),
                   jax.ShapeDtypeStruct((8, LANES), F32)],
        scratch_shapes=[pltpu.VMEM((1, LANES), F32)],
        compiler_params=_cparams(("arbitrary",), 32),
    )(x2, nw.reshape(1, D_MODEL), jnp.concatenate([whi, wlo], axis=1), br)


def _moe_plan(e1, e2, r1, r2, cnt, tm_e):
    n = e1.shape[0]
    ar = jnp.arange(N_EXPERTS, dtype=jnp.int32)
    offs = jnp.cumsum(cnt) - cnt
    dest = jnp.stack([jnp.take(offs, e1) + r1, jnp.take(offs, e2) + r2])
    n_tiles = (2 * n) // tm_e
    first = offs // tm_e
    last = jnp.where(cnt > 0, (offs + cnt - 1) // tm_e, first - 1)
    n_items = last - first + 1
    item_end = jnp.cumsum(n_items)
    item_start = item_end - n_items
    n_work = n_tiles + N_EXPERTS - 1
    w = jnp.arange(n_work, dtype=jnp.int32)
    e_w = jnp.minimum(jnp.sum((w[:, None] >= item_end[None, :]).astype(jnp.int32), axis=1), N_EXPERTS - 1)
    valid = w < item_end[-1]
    ohw = (e_w[:, None] == ar[None, :]).astype(jnp.int32)
    sel = lambda tab: jnp.sum(ohw * tab[None, :], axis=1)
    tile = sel(first) + (w - sel(item_start))
    lo = jnp.maximum(sel(offs), tile * tm_e)
    hi = jnp.minimum(sel(offs) + sel(cnt), (tile + 1) * tm_e)
    last_e = jnp.max(jnp.where(n_items > 0, ar, 0))
    tile = jnp.where(valid, tile, n_tiles - 1)
    e_w = jnp.where(valid, e_w, last_e)
    lo = jnp.where(valid, lo, 0)
    hi = jnp.where(valid, hi, 0)
    return dest.astype(jnp.int32), tile.astype(jnp.int32), e_w.astype(jnp.int32), lo.astype(jnp.int32), hi.astype(jnp.int32)


def _row_copy(src_ref, src_row, dst_ref, dst_row, sem):
    return pltpu.make_async_copy(src_ref.at[pl.ds(src_row, 1), :], dst_ref.at[pl.ds(dst_row, 1), :], sem)


ROW_UNROLL = 8


def _dispatch_kernel(dest_ref, x_ref, xs_ref, sem, *, tm):
    def copies(i):
        for u in range(ROW_UNROLL):
            r = i * ROW_UNROLL + u
            for k in range(2):
                yield _row_copy(x_ref, r, xs_ref, dest_ref[k, r], sem)

    def issue(i, carry):
        for cp in copies(i):
            cp.start()
        return carry

    def drain(i, carry):
        for cp in copies(i):
            cp.wait()
        return carry

    lax.fori_loop(0, tm // ROW_UNROLL, issue, 0)
    lax.fori_loop(0, tm // ROW_UNROLL, drain, 0)


def _dispatch(x2, dest3, tm):
    n = x2.shape[0]
    return pl.pallas_call(
        functools.partial(_dispatch_kernel, tm=tm),
        grid=(n // tm,),
        in_specs=[pl.BlockSpec((None, 2, tm), lambda i: (i, 0, 0), memory_space=pltpu.SMEM),
                  pl.BlockSpec((tm, D_MODEL), lambda i: (i, 0))],
        out_specs=pl.BlockSpec(memory_space=pl.ANY),
        out_shape=jax.ShapeDtypeStruct((2 * n, D_MODEL), F32),
        scratch_shapes=[pltpu.SemaphoreType.DMA(())],
        compiler_params=_cparams(("arbitrary",), 32),
    )(dest3, x2)


def _experts_kernel(tile_ref, exp_ref, lo_ref, hi_ref, xs_ref, nw_ref, wg_ref, wu_ref, wd_ref, ys_ref, *, tm):
    w = pl.program_id(0)
    lo = lo_ref[w]
    hi = hi_ref[w]
    tile = tile_ref[w]
    first_visit = jnp.logical_or(w == 0, tile_ref[jnp.maximum(w - 1, 0)] != tile)

    @pl.when(hi > lo)
    def _():
        xn = _rms(xs_ref[...], nw_ref[...]).astype(BF16)
        gate = jnp.dot(xn, wg_ref[...], preferred_element_type=F32)
        up = jnp.dot(xn, wu_ref[...], preferred_element_type=F32)
        h = (gate * jax.nn.sigmoid(gate) * up).astype(BF16)
        y = jnp.dot(h, wd_ref[...], preferred_element_type=F32)

        @pl.when(first_visit)
        def _():
            ys_ref[...] = y

        @pl.when(jnp.logical_not(first_visit))
        def _():
            rows = tile * tm + lax.broadcasted_iota(jnp.int32, (tm, 1), 0)
            ys_ref[...] = jnp.where((rows >= lo) & (rows < hi), y, ys_ref[...])


def _experts(xs, nw, wg_bf, wu_bf, wd_bf, tile, exp, lo, hi, tm):
    n_work = tile.shape[0]
    grid_spec = pltpu.PrefetchScalarGridSpec(
        num_scalar_prefetch=4,
        grid=(n_work,),
        in_specs=[pl.BlockSpec((tm, D_MODEL), lambda w, t, e, lo, hi: (t[w], 0)),
                  pl.BlockSpec((1, D_MODEL), lambda w, t, e, lo, hi: (0, 0)),
                  pl.BlockSpec((None, D_MODEL, D_EXPERT), lambda w, t, e, lo, hi: (e[w], 0, 0)),
                  pl.BlockSpec((None, D_MODEL, D_EXPERT), lambda w, t, e, lo, hi: (e[w], 0, 0)),
                  pl.BlockSpec((None, D_EXPERT, D_MODEL), lambda w, t, e, lo, hi: (e[w], 0, 0))],
        out_specs=pl.BlockSpec((tm, D_MODEL), lambda w, t, e, lo, hi: (t[w], 0)),
    )
    return pl.pallas_call(
        functools.partial(_experts_kernel, tm=tm),
        grid_spec=grid_spec,
        out_shape=jax.ShapeDtypeStruct(xs.shape, F32),
        compiler_params=_cparams(("arbitrary",), 48),
    )(tile, exp, lo, hi, xs, nw.reshape(1, D_MODEL), wg_bf, wu_bf, wd_bf)


def _combine_kernel(dest_ref, route_ref, x_ref, ys_ref, o_ref, buf0, buf1, sem, *, tm):
    def copies(i):
        for u in range(ROW_UNROLL):
            r = i * ROW_UNROLL + u
            yield _row_copy(ys_ref, dest_ref[0, r], buf0, r, sem)
            yield _row_copy(ys_ref, dest_ref[1, r], buf1, r, sem)

    def issue(i, carry):
        for cp in copies(i):
            cp.start()
        return carry

    def drain(i, carry):
        for cp in copies(i):
            cp.wait()
        return carry

    lax.fori_loop(0, tm // ROW_UNROLL, issue, 0)
    lax.fori_loop(0, tm // ROW_UNROLL, drain, 0)
    route = route_ref[...]
    o_ref[...] = x_ref[...] + route[:, 2:3] * buf0[...] + route[:, 3:4] * buf1[...]


def _combine(x2, route, ys, dest3, tm):
    n = x2.shape[0]
    return pl.pallas_call(
        functools.partial(_combine_kernel, tm=tm),
        grid=(n // tm,),
        in_specs=[pl.BlockSpec((None, 2, tm), lambda i: (i, 0, 0), memory_space=pltpu.SMEM),
                  pl.BlockSpec((tm, LANES), lambda i: (i, 0)),
                  pl.BlockSpec((tm, D_MODEL), lambda i: (i, 0)),
                  pl.BlockSpec(memory_space=pl.ANY)],
        out_specs=pl.BlockSpec((tm, D_MODEL), lambda i: (i, 0)),
        out_shape=jax.ShapeDtypeStruct((n, D_MODEL), F32),
        scratch_shapes=[pltpu.VMEM((tm, D_MODEL), F32), pltpu.VMEM((tm, D_MODEL), F32),
                        pltpu.SemaphoreType.DMA(())],
        compiler_params=_cparams(("arbitrary",), 32),
    )(dest3, route, x2, ys)


def _ffn(x2, lw):
    n = x2.shape[0]
    route, route_t, cnt = _router(x2, lw["norm_ffn_w"], lw["w_rg"], lw["b_rg"], lw["w_re"], lw["b_re"])
    fields = route_t.astype(jnp.int32)
    tm_e = _pick(2 * n, 512) if n >= 4096 else 128
    tm_r = _pick(n, 256)
    dest, tile, exp, lo, hi = _moe_plan(fields[0], fields[1], fields[4], fields[5],
                                        cnt[0, :N_EXPERTS].astype(jnp.int32), tm_e)
    dest3 = dest.reshape(2, n // tm_r, tm_r).transpose(1, 0, 2)
    xs = _dispatch(x2, dest3, tm_r)
    ys = _experts(xs, lw["norm_ffn_w"], lw["wg"], lw["wu"], lw["wd"], tile, exp, lo, hi, tm_e)
    return _combine(x2, route, ys, dest3, tm_r)


def _layer(x3, pos, l, lw, ret_state, kv_cache):
    bsz, seq, _ = x3.shape
    n = bsz * seq
    sample = ret_state is not None
    lam_init = 0.8 - 0.6 * math.exp(-0.3 * l)
    x2 = x3.reshape(n, D_MODEL)
    proj2 = _inproj(x2, lw["norm_mix_w"], lw["w_in"])
    proj3 = proj2.reshape(bsz, seq, D_IN)

    a_y, a_v = _gmlp(proj2, seq, lw["a_norm_w"], lw["a_ws"], lw["a_bs_b"], emit_v=sample)
    b_y, s_new = _retention(proj3, pos, lw["b_norm_w"], ret_state)
    prep = _cprep(proj3, lw["c_qnorm_w"], lw["c_knorm_w"], lw["c_lambda"], lam_init, emit_vt=not sample)
    qn, kn, kf, vf, lam_b = prep[:5]
    if sample:
        c_y = _dattn_sample(qn, kn, proj3, kv_cache[0], kv_cache[1], lam_b, lw["c_subln_w"], 1.0 - lam_init)
    else:
        c_y = _dattn_prompt(qn, kn, prep[5], lam_b, lw["c_subln_w"], 1.0 - lam_init)

    u = _merge(a_y, b_y.reshape(n, A_WIDTH), c_y.reshape(n, A_WIDTH), proj2, lw["w_branch"])
    x2 = _outproj(u, lw["w_out"], x2)
    x2 = _ffn(x2, lw)
    new_k = kf.reshape(bsz, seq, C_HEADS, 2 * C_DH)
    new_v = vf.reshape(bsz, seq, C_HEADS, C_DV)
    return x2.reshape(bsz, seq, D_MODEL), new_k, new_v, s_new, a_v


def kernel(x_prompt, x_sample, cache_k_c, cache_v_c, state_ret, norm_mix_w, w_in, a_norm_w, a_ws, a_bs, b_norm_w, c_qnorm_w, c_knorm_w, c_lambda, c_subln_w, w_branch, w_out, norm_ffn_w, w_router_group, b_router_group, w_router_expert, b_router_expert, w_gate_e, w_up_e, w_down_e):
    depth = w_in.shape[0]
    past = cache_k_c.shape[2]
    pos_p = jnp.arange(x_prompt.shape[1])
    pos_s = past + jnp.arange(x_sample.shape[1])
    yp, ys = x_prompt, x_sample
    outs = [[] for _ in range(7)]
    for l in range(depth):
        lw = dict(
            norm_mix_w=norm_mix_w[l], w_in=w_in[l].astype(BF16), a_norm_w=a_norm_w[l], a_ws=a_ws[l],
            a_bs_b=jnp.broadcast_to(a_bs[l][:, :, None], (A_GROUPS, A_BLOCK, A_GROUP)),
            b_norm_w=b_norm_w[l], c_qnorm_w=c_qnorm_w[l], c_knorm_w=c_knorm_w[l], c_lambda=c_lambda[l],
            c_subln_w=c_subln_w[l], w_branch=w_branch[l].astype(BF16), w_out=w_out[l].astype(BF16),
            norm_ffn_w=norm_ffn_w[l], w_rg=w_router_group[l], b_rg=b_router_group[l],
            w_re=w_router_expert[l], b_re=b_router_expert[l],
            wg=w_gate_e[l].astype(BF16), wu=w_up_e[l].astype(BF16), wd=w_down_e[l].astype(BF16))
        yp, kp, vp, rp, _ = _layer(yp, pos_p, l, lw, None, None)
        ys, kn, vn, rn, avn = _layer(ys, pos_s, l, lw, state_ret[l], (cache_k_c[l], cache_v_c[l]))
        for lst, val in zip(outs, (kp, vp, rp, kn, vn, rn, avn.reshape(ys.shape[0], ys.shape[1], A_WIDTH))):
            lst.append(val)
    return (yp, ys) + tuple(jnp.stack(o, 0) for o in outs)
```

```python
import functools
import math

import jax
import jax.numpy as jnp
from jax import lax
from jax.experimental import pallas as pl
from jax.experimental.pallas import tpu as pltpu

F32 = jnp.float32
BF16 = jnp.bfloat16

D_MODEL = 2048
CHUNK = 64
A_WIDTH = 1024
A_BLOCK = 128
A_GROUP = 128
A_GROUPS = 8
B_HEADS = 4
B_DK = 128
B_DV = 256
ROPE_BASE = 10000.0
C_HEADS = 8
C_DH = 64
C_DV = 128
N_GROUPS = 4
EXPERTS_PER_GROUP = 4
N_EXPERTS = 16
D_EXPERT = 512
EPS = 1e-6
D_IN = 14336

OFF_AU, OFF_AV, OFF_BQ, OFF_BK, OFF_BV, OFF_BG, OFF_CQ, OFF_CK, OFF_CV, OFF_GATE = (
    0, 1024, 2048, 2560, 3072, 4096, 5120, 6144, 7168, 8192)

LANES = 128
MIB = 1024 * 1024
NEG_BIG = -1e30
Q_SCALE = (C_DH ** -0.5) * math.log2(math.e)
VT_ROWS = C_DV + 16


def _cparams(sem, vmem_mib):
    return pltpu.CompilerParams(dimension_semantics=sem, vmem_limit_bytes=vmem_mib * MIB)


def _rms(x, w):
    ms = jnp.mean(x * x, axis=-1, keepdims=True)
    return x * lax.rsqrt(ms + EPS) * w


def _pick(n, pref):
    t = min(pref, n)
    while n % t:
        t //= 2
    return t


def _inproj_kernel(x_ref, g_ref, w_ref, o_ref, xn_ref):
    @pl.when(pl.program_id(1) == 0)
    def _():
        xn_ref[...] = _rms(x_ref[...], g_ref[...]).astype(BF16)

    o_ref[...] = jnp.dot(xn_ref[...], w_ref[...], preferred_element_type=F32).astype(o_ref.dtype)


def _inproj(x2, g, w_bf, l):
    n = x2.shape[0]
    tm = _pick(n, 1024)
    tn = 1024
    return pl.pallas_call(
        _inproj_kernel,
        grid=(n // tm, D_IN // tn),
        in_specs=[pl.BlockSpec((tm, D_MODEL), lambda i, j: (i, 0)),
                  pl.BlockSpec((1, D_MODEL), lambda i, j: (0, 0)),
                  pl.BlockSpec((None, D_MODEL, tn), lambda i, j: (l, 0, j))],
        out_specs=pl.BlockSpec((tm, tn), lambda i, j: (i, j)),
        out_shape=jax.ShapeDtypeStruct((n, D_IN), BF16),
        scratch_shapes=[pltpu.VMEM((tm, D_MODEL), BF16)],
        compiler_params=_cparams(("parallel", "arbitrary"), 48),
    )(x2, g.reshape(1, D_MODEL), w_bf)


def _gmlp_kernel(u_ref, v_ref, nw_ref, ws_ref, bs_ref, y_ref, *rest, pb, nblk, emit_v):
    u = jax.nn.gelu(u_ref[...].astype(F32))
    v = _rms(jax.nn.gelu(v_ref[...].astype(F32)), nw_ref[...])
    if emit_v:
        rest[0][...] = v
    vb = v.astype(BF16)
    p = lax.broadcasted_iota(jnp.int32, (A_BLOCK, A_BLOCK), 0)
    q = lax.broadcasted_iota(jnp.int32, (A_BLOCK, A_BLOCK), 1)
    mask = (q // CHUNK) <= (p // CHUNK)
    for g in range(A_GROUPS):
        w = jnp.where(mask, ws_ref[g], 0.0).astype(BF16)
        bias = bs_ref[g, :pb, :]
        cs = slice(g * A_GROUP, (g + 1) * A_GROUP)
        for b in range(nblk):
            rs = slice(b * pb, (b + 1) * pb)
            vg = vb[rs, cs]
            if pb < A_BLOCK:
                vg = jnp.concatenate([vg, jnp.zeros((A_BLOCK - pb, A_GROUP), BF16)], axis=0)
            s = jnp.dot(w, vg, preferred_element_type=F32)[:pb] + bias
            y_ref[rs, cs] = (u[rs, cs] * s).astype(y_ref.dtype)


def _gmlp(proj2, seq_len, nw, ws, bs_b, emit_v):
    n = proj2.shape[0]
    if seq_len % A_BLOCK == 0:
        pb, nblk = A_BLOCK, 4 if seq_len % (4 * A_BLOCK) == 0 else 1
    else:
        assert seq_len <= A_BLOCK
        pb, nblk = seq_len, 1
    r = pb * nblk
    out_shape = [jax.ShapeDtypeStruct((n, A_WIDTH), BF16)]
    out_specs = [pl.BlockSpec((r, A_WIDTH), lambda i: (i, 0))]
    if emit_v:
        out_shape.append(jax.ShapeDtypeStruct((n, A_WIDTH), F32))
        out_specs.append(pl.BlockSpec((r, A_WIDTH), lambda i: (i, 0)))
    res = pl.pallas_call(
        functools.partial(_gmlp_kernel, pb=pb, nblk=nblk, emit_v=emit_v),
        grid=(n // r,),
        in_specs=[pl.BlockSpec((r, A_WIDTH), lambda i: (i, OFF_AU // A_WIDTH)),
                  pl.BlockSpec((r, A_WIDTH), lambda i: (i, OFF_AV // A_WIDTH)),
                  pl.BlockSpec((1, A_WIDTH), lambda i: (0, 0)),
                  pl.BlockSpec((A_GROUPS, A_BLOCK, A_BLOCK), lambda i: (0, 0, 0)),
                  pl.BlockSpec((A_GROUPS, A_BLOCK, A_GROUP), lambda i: (0, 0, 0))],
        out_specs=out_specs,
        out_shape=out_shape,
        compiler_params=_cparams(("parallel",), 32),
    )(proj2, proj2, nw.reshape(1, A_WIDTH), ws, bs_b)
    return res if emit_v else (res[0], None)


def _ret_kernel(q_ref, k_ref, v_ref, g_ref, cos_ref, sin_ref, dm_ref, qd_ref, kd_ref, sd_ref, nw_ref, *rest,
                has_init):
    if has_init:
        s0_ref, y_ref, sfin_ref, s_scr = rest
    else:
        y_ref, sfin_ref, s_scr = rest
    c = pl.program_id(1)

    @pl.when(c == 0)
    def _():
        s_scr[...] = s0_ref[...] if has_init else jnp.zeros_like(s_scr)

    cos = cos_ref[...]
    sin = sin_ref[...]
    nw = nw_ref[...]
    for h in range(B_HEADS):
        ks = slice(h * B_DK, (h + 1) * B_DK)
        vs = slice(h * B_DV, (h + 1) * B_DV)
        q = q_ref[:, ks].astype(F32)
        k = k_ref[:, ks].astype(F32)
        qr = q * cos + pltpu.roll(q, B_DK // 2, 1) * sin
        kr = (k * cos + pltpu.roll(k, B_DK // 2, 1) * sin) * (B_DK ** -0.5)
        qb = qr.astype(BF16)
        kb = kr.astype(BF16)
        v = v_ref[:, vs]
        attn = lax.dot_general(qb, kb, (((1,), (1,)), ((), ())), preferred_element_type=F32) * dm_ref[h]
        intra = jnp.dot(attn.astype(BF16), v, preferred_element_type=F32)
        s_prev = s_scr[h]
        inter = jnp.dot(qb, s_prev.astype(BF16), preferred_element_type=F32) * qd_ref[h]
        o = intra + inter
        kt = jnp.transpose(kr * kd_ref[h]).astype(BF16)
        s_scr[h] = sd_ref[h] * s_prev + jnp.dot(kt, v, preferred_element_type=F32)
        gate = g_ref[:, vs].astype(F32)
        y_ref[:, vs] = (_rms(o, nw) * (gate * jax.nn.sigmoid(gate))).astype(y_ref.dtype)

    @pl.when(c == pl.num_programs(1) - 1)
    def _():
        sfin_ref[...] = s_scr[...]


def _ret_tables(c):
    lg = jnp.log1p(-jnp.exp2(-5.0 - jnp.arange(B_HEADS, dtype=F32)))
    idx = jnp.arange(c, dtype=F32)
    diff = idx[:, None] - idx[None, :]
    dm = jnp.where(diff >= 0, jnp.exp(jnp.maximum(diff, 0.0)[None] * lg[:, None, None]), 0.0)
    qd = jnp.exp((idx + 1.0)[None, :] * lg[:, None])
    kd = jnp.exp((c - 1.0 - idx)[None, :] * lg[:, None])
    sd = jnp.exp(c * lg)
    return (dm, jnp.broadcast_to(qd[:, :, None], (B_HEADS, c, B_DV)),
            jnp.broadcast_to(kd[:, :, None], (B_HEADS, c, B_DK)),
            jnp.broadcast_to(sd[:, None, None], (B_HEADS, 1, B_DV)))


def _rope_tables(pos):
    half = B_DK // 2
    inv_freq = 1.0 / (ROPE_BASE ** jnp.linspace(0.0, 1.0, half, dtype=F32))
    ang = pos.astype(F32)[:, None] * inv_freq[None, :]
    cos, sin = jnp.cos(ang), jnp.sin(ang)
    return jnp.concatenate([cos, cos], axis=-1), jnp.concatenate([-sin, sin], axis=-1)


def _retention(proj3, pos, nw, s0):
    bsz, seq, _ = proj3.shape
    c = _pick(seq, 256)
    dm, qd, kd, sd = _ret_tables(c)
    cos, sin = _rope_tables(pos)
    has_init = s0 is not None
    qk_w = B_HEADS * B_DK
    v_w = B_HEADS * B_DV
    full = lambda shape: pl.BlockSpec(shape, lambda b, i: (0,) * len(shape))
    state = pl.BlockSpec((None, B_HEADS, B_DK, B_DV), lambda b, i: (b, 0, 0, 0))
    in_specs = [pl.BlockSpec((None, c, qk_w), lambda b, i: (b, i, OFF_BQ // qk_w)),
                pl.BlockSpec((None, c, qk_w), lambda b, i: (b, i, OFF_BK // qk_w)),
                pl.BlockSpec((None, c, v_w), lambda b, i: (b, i, OFF_BV // v_w)),
                pl.BlockSpec((None, c, v_w), lambda b, i: (b, i, OFF_BG // v_w)),
                pl.BlockSpec((c, B_DK), lambda b, i: (i, 0)),
                pl.BlockSpec((c, B_DK), lambda b, i: (i, 0)),
                full((B_HEADS, c, c)), full((B_HEADS, c, B_DV)), full((B_HEADS, c, B_DK)),
                full((B_HEADS, 1, B_DV)), full((1, B_DV))]
    args = [proj3, proj3, proj3, proj3, cos, sin, dm, qd, kd, sd, nw.reshape(1, B_DV)]
    if has_init:
        in_specs.append(state)
        args.append(s0)
    return pl.pallas_call(
        functools.partial(_ret_kernel, has_init=has_init),
        grid=(bsz, seq // c),
        in_specs=in_specs,
        out_specs=[pl.BlockSpec((None, c, v_w), lambda b, i: (b, i, 0)), state],
        out_shape=[jax.ShapeDtypeStruct((bsz, seq, v_w), BF16),
                   jax.ShapeDtypeStruct((bsz, B_HEADS, B_DK, B_DV), F32)],
        scratch_shapes=[pltpu.VMEM((B_HEADS, B_DK, B_DV), F32)],
        compiler_params=_cparams(("parallel", "arbitrary"), 32),
    )(*args)


def _group_mean_matrix():
    r = lax.broadcasted_iota(jnp.int32, (LANES, LANES), 0)
    c = lax.broadcasted_iota(jnp.int32, (LANES, LANES), 1)
    return jnp.where((r // C_DH) == (c // C_DH), 1.0 / C_DH, 0.0).astype(BF16)


def _cprep_kernel(q_ref, k_ref, v_ref, qw_ref, kw_ref, lam_ref, *rest, lam_init, emit_vt, n_prev, l):
    qn_ref, kn_ref, kf_ref, vf_ref, lamo_ref = rest[n_prev:n_prev + 5]
    rows = q_ref.shape[0]
    if n_prev == 0:
        for d in range(kf_ref.shape[0]):
            if d != l:
                kf_ref[d] = jnp.zeros(kf_ref.shape[1:], F32)
                vf_ref[d] = jnp.zeros(vf_ref.shape[1:], F32)
        kf_ref, vf_ref = kf_ref.at[l], vf_ref.at[l]
    gm = _group_mean_matrix()
    qw = qw_ref[...]
    kw = kw_ref[...]
    for h in range(C_HEADS):
        cs = slice(h * LANES, (h + 1) * LANES)
        hs = pl.ds(h, rows, stride=C_HEADS)
        x = q_ref[:, cs].astype(F32)
        ms = jnp.dot((x * x).astype(BF16), gm, preferred_element_type=F32)
        qn_ref[:, cs] = (x * lax.rsqrt(ms + EPS) * qw * Q_SCALE).astype(BF16)
        x = k_ref[:, cs].astype(F32)
        ms = jnp.dot((x * x).astype(BF16), gm, preferred_element_type=F32)
        kn = x * lax.rsqrt(ms + EPS) * kw
        kf_ref[hs, :] = kn
        kn_ref[:, cs] = kn.astype(BF16)
        vv = v_ref[:, cs].astype(F32)
        vf_ref[hs, :] = vv
        if emit_vt:
            vt_ref = rest[n_prev + 5]
            vt_ref[h, :C_DV, :] = jnp.transpose(vv).astype(BF16)
            vt_ref[h, C_DV:, :] = jnp.ones((VT_ROWS - C_DV, rows), BF16)
    lq = lam_ref[...]
    l01 = jnp.sum(lq[0:1] * lq[1:2], axis=-1, keepdims=True)
    l23 = jnp.sum(lq[2:3] * lq[3:4], axis=-1, keepdims=True)
    lam = jnp.exp(l01) - jnp.exp(l23) + lam_init
    lamo_ref[...] = jnp.broadcast_to(lam, lamo_ref.shape)


def _cprep(proj3, qw, kw, lam_p, lam_init, emit_vt, l, depth, prev_kv):
    bsz, seq, _ = proj3.shape
    r = _pick(seq, 512)
    w = C_HEADS * LANES
    qw2 = jnp.concatenate([qw, qw]).reshape(1, LANES)
    kw2 = jnp.concatenate([kw, kw]).reshape(1, LANES)
    blk = lambda off: pl.BlockSpec((None, r, w), lambda b, i: (b, i, off // w))
    row = pl.BlockSpec((None, r, w), lambda b, i: (b, i, 0))
    if prev_kv is None:
        kv = pl.BlockSpec((depth, None, r * C_HEADS, LANES), lambda b, i: (0, b, i, 0))
    else:
        kv = pl.BlockSpec((None, None, r * C_HEADS, LANES), lambda b, i: (l, b, i, 0))
    kv_shape = jax.ShapeDtypeStruct((depth, bsz, seq * C_HEADS, LANES), F32)
    out_shape = [jax.ShapeDtypeStruct((bsz, seq, w), BF16), jax.ShapeDtypeStruct((bsz, seq, w), BF16),
                 kv_shape, kv_shape, jax.ShapeDtypeStruct((8, LANES), F32)]
    out_specs = [row, row, kv, kv, pl.BlockSpec((8, LANES), lambda b, i: (0, 0))]
    if emit_vt:
        out_shape.append(jax.ShapeDtypeStruct((bsz, C_HEADS, VT_ROWS, seq), BF16))
        out_specs.append(pl.BlockSpec((None, C_HEADS, VT_ROWS, r), lambda b, i: (b, 0, 0, i)))
    in_specs = [blk(OFF_CQ), blk(OFF_CK), blk(OFF_CV),
                pl.BlockSpec((1, LANES), lambda b, i: (0, 0)),
                pl.BlockSpec((1, LANES), lambda b, i: (0, 0)),
                pl.BlockSpec((4, C_DH), lambda b, i: (0, 0))]
    args = [proj3, proj3, proj3, qw2, kw2, lam_p]
    aliases = {}
    if prev_kv is not None:
        in_specs += [pl.BlockSpec(memory_space=pl.ANY)] * 2
        aliases = {len(args): 2, len(args) + 1: 3}
        args += list(prev_kv)
    return pl.pallas_call(
        functools.partial(_cprep_kernel, lam_init=lam_init, emit_vt=emit_vt, n_prev=len(aliases), l=l),
        grid=(bsz, seq // r),
        in_specs=in_specs,
        out_specs=out_specs,
        out_shape=out_shape,
        input_output_aliases=aliases,
        compiler_params=_cparams(("arbitrary", "arbitrary"), 40),
    )(*args)


def _dattn_kernel(q_ref, k_ref, vt_ref, lam_ref, sw_ref, o_ref, s_a, s_b, m_scr, a_scr, *, t, out_scale):
    qi = pl.program_id(2)
    q = q_ref[...]
    lane = lax.broadcasted_iota(jnp.int32, q.shape, 1)
    qc = (jnp.where(lane < C_DH, q, jnp.zeros_like(q)), jnp.where(lane >= C_DH, q, jnp.zeros_like(q)))
    m_scr[...] = jnp.full(m_scr.shape, NEG_BIG, F32)
    a_scr[...] = jnp.zeros(a_scr.shape, F32)

    def scores(blk, s_ref):
        kb = k_ref[pl.ds(pl.multiple_of(blk * t, t), t), :]
        for c in range(2):
            s_ref[c] = lax.dot_general(kb, qc[c], (((1,), (1,)), ((), ())), preferred_element_type=F32)

    def update(blk, s_ref, masked=False):
        vtb = vt_ref[:, pl.ds(pl.multiple_of(blk * t, t), t)]
        if masked:
            kpos = lax.broadcasted_iota(jnp.int32, (t, t), 0)
            qpos = lax.broadcasted_iota(jnp.int32, (t, t), 1)
            vis = (kpos // CHUNK) <= (qpos // CHUNK)
        for c in range(2):
            s = s_ref[c]
            if masked:
                s = jnp.where(vis, s, NEG_BIG)
            m_old = m_scr[c]
            m_new = jnp.maximum(m_old, jnp.max(s, axis=0, keepdims=True))
            alpha = jnp.exp2(m_old - m_new)
            p = jnp.exp2(s - m_new).astype(BF16)
            a_scr[c] = alpha * a_scr[c] + jnp.dot(vtb, p, preferred_element_type=F32)
            m_scr[c] = m_new

    scores(qi, s_a)

    @pl.when(qi == 0)
    def _():
        update(qi, s_a, masked=True)

    @pl.when(qi > 0)
    def _():
        scores(0, s_b)
        update(qi, s_a, masked=True)

        def pair(jj, carry):
            j = 2 * jj
            scores(j + 1, s_a)
            update(j, s_b)
            scores(j + 2, s_b)
            update(j + 1, s_a)
            return carry

        lax.fori_loop(0, (qi - 1) // 2, pair, 0)

        @pl.when(qi % 2 == 1)
        def _():
            update(qi - 1, s_b)

        @pl.when(qi % 2 == 0)
        def _():
            scores(qi - 1, s_a)
            update(qi - 2, s_b)
            update(qi - 1, s_a)

    lam = lam_ref[0:1, 0:1]
    o_t = (a_scr[0, :C_DV, :] / a_scr[0, C_DV:C_DV + 1, :]
           - lam * (a_scr[1, :C_DV, :] / a_scr[1, C_DV:C_DV + 1, :]))
    ms = jnp.mean(o_t * o_t, axis=0, keepdims=True)
    o = jnp.transpose(o_t * lax.rsqrt(ms + EPS))
    o_ref[...] = (o * sw_ref[...] * out_scale).astype(o_ref.dtype)


def _dattn_prompt(qn, kn, vt, lam_b, sw, out_scale):
    bsz, seq, w = qn.shape
    t = _pick(seq, 512)
    return pl.pallas_call(
        functools.partial(_dattn_kernel, t=t, out_scale=out_scale),
        grid=(bsz, C_HEADS, seq // t),
        in_specs=[pl.BlockSpec((None, t, LANES), lambda b, h, i: (b, i, h)),
                  pl.BlockSpec((None, seq, LANES), lambda b, h, i: (b, 0, h)),
                  pl.BlockSpec((None, None, VT_ROWS, seq), lambda b, h, i: (b, h, 0, 0)),
                  pl.BlockSpec((8, LANES), lambda b, h, i: (0, 0)),
                  pl.BlockSpec((1, C_DV), lambda b, h, i: (0, 0))],
        out_specs=pl.BlockSpec((None, t, C_DV), lambda b, h, i: (b, i, h)),
        out_shape=jax.ShapeDtypeStruct((bsz, seq, w), BF16),
        scratch_shapes=[pltpu.VMEM((2, t, t), F32), pltpu.VMEM((2, t, t), F32),
                        pltpu.VMEM((2, 1, t), F32), pltpu.VMEM((2, VT_ROWS, t), F32)],
        compiler_params=_cparams(("parallel", "parallel", "arbitrary"), 40),
    )(qn, kn, vt, lam_b, sw.reshape(1, C_DV))


def _dattn_sample_kernel(q_ref, kc_ref, vc_ref, kn_ref, vn_ref, lam_ref, sw_ref, o_ref, *, past, out_scale):
    lq = q_ref.shape[0]
    lane = lax.broadcasted_iota(jnp.int32, (lq, LANES), 1)
    qpos_c = past + lax.broadcasted_iota(jnp.int32, (lq, past), 0)
    kpos_c = lax.broadcasted_iota(jnp.int32, (lq, past), 1)
    vis_c = (kpos_c // CHUNK) <= (qpos_c // CHUNK)
    qpos_n = past + lax.broadcasted_iota(jnp.int32, (lq, lq), 0)
    kpos_n = past + lax.broadcasted_iota(jnp.int32, (lq, lq), 1)
    vis_n = (kpos_n // CHUNK) <= (qpos_n // CHUNK)
    lam = lam_ref[0:1, 0:1]
    nt = (((1,), (1,)), ((), ()))
    for h in range(C_HEADS):
        cs = slice(h * LANES, (h + 1) * LANES)
        q = q_ref[:, cs]
        kc = kc_ref[pl.ds(h, past, stride=C_HEADS), :].astype(BF16)
        vc = vc_ref[pl.ds(h, past, stride=C_HEADS), :].astype(BF16)
        kn = kn_ref[:, cs]
        vn = vn_ref[:, cs]
        outs = []
        for c in range(2):
            qc = jnp.where((lane // C_DH) == c, q, jnp.zeros_like(q))
            s_c = jnp.where(vis_c, lax.dot_general(qc, kc, nt, preferred_element_type=F32), NEG_BIG)
            s_n = jnp.where(vis_n, lax.dot_general(qc, kn, nt, preferred_element_type=F32), NEG_BIG)
            m = jnp.maximum(jnp.max(s_c, axis=-1, keepdims=True), jnp.max(s_n, axis=-1, keepdims=True))
            p_c = jnp.exp2(s_c - m)
            p_n = jnp.exp2(s_n - m)
            l = jnp.sum(p_c, axis=-1, keepdims=True) + jnp.sum(p_n, axis=-1, keepdims=True)
            acc = (jnp.dot(p_c.astype(BF16), vc, preferred_element_type=F32)
                   + jnp.dot(p_n.astype(BF16), vn, preferred_element_type=F32))
            outs.append(acc / l)
        o = outs[0] - lam * outs[1]
        o_ref[:, cs] = (_rms(o, sw_ref[...]) * out_scale).astype(o_ref.dtype)


def _dattn_sample(qn, kn, proj3, cache_k, cache_v, l, lam_b, sw, out_scale):
    bsz, lq, w = qn.shape
    depth, _, past = cache_k.shape[:3]
    ck = cache_k.reshape(depth, bsz, past * C_HEADS, LANES)
    cv = cache_v.reshape(depth, bsz, past * C_HEADS, LANES)
    rows = lambda r: pl.BlockSpec((None, r, w), lambda b: (b, 0, 0))
    cache = pl.BlockSpec((None, None, past * C_HEADS, LANES), lambda b: (l, b, 0, 0))
    return pl.pallas_call(
        functools.partial(_dattn_sample_kernel, past=past, out_scale=out_scale),
        grid=(bsz,),
        in_specs=[rows(lq), cache, cache, rows(lq),
                  pl.BlockSpec((None, lq, w), lambda b: (b, 0, OFF_CV // w)),
                  pl.BlockSpec((8, LANES), lambda b: (0, 0)),
                  pl.BlockSpec((1, C_DV), lambda b: (0, 0))],
        out_specs=rows(lq),
        out_shape=jax.ShapeDtypeStruct((bsz, lq, w), BF16),
        compiler_params=_cparams(("parallel",), 40),
    )(qn, ck, cv, kn, proj3, lam_b, sw.reshape(1, C_DV))


def _merge_kernel(ya_ref, yb_ref, yc_ref, ga_ref, gb_ref, gc_ref, w_ref, o_ref):
    acc = None
    for y_ref, g_ref, n in ((ya_ref, ga_ref, 0), (yb_ref, gb_ref, 1), (yc_ref, gc_ref, 2)):
        up = jnp.dot(y_ref[...], w_ref[n], preferred_element_type=F32)
        term = jax.nn.sigmoid(g_ref[...].astype(F32)) * up
        acc = term if acc is None else acc + term
    o_ref[...] = acc.astype(o_ref.dtype)


def _merge(ya, yb, yc, proj2, wb_bf, l):
    n = ya.shape[0]
    tm = _pick(n, 512)
    tn = 1024
    ybs = pl.BlockSpec((tm, A_WIDTH), lambda j, i: (i, 0))
    gate = lambda b: pl.BlockSpec((tm, tn), lambda j, i: (i, (OFF_GATE + b * D_MODEL) // tn + j))
    return pl.pallas_call(
        _merge_kernel,
        grid=(D_MODEL // tn, n // tm),
        in_specs=[ybs, ybs, ybs, gate(0), gate(1), gate(2),
                  pl.BlockSpec((None, 3, A_WIDTH, tn), lambda j, i: (l, 0, 0, j))],
        out_specs=pl.BlockSpec((tm, tn), lambda j, i: (i, j)),
        out_shape=jax.ShapeDtypeStruct((n, D_MODEL), BF16),
        compiler_params=_cparams(("parallel", "parallel"), 48),
    )(ya, yb, yc, proj2, proj2, proj2, wb_bf)


def _outproj_kernel(u_ref, w_ref, x_ref, o_ref):
    o_ref[...] = x_ref[...] + jnp.dot(u_ref[...], w_ref[...], preferred_element_type=F32)


def _outproj(u, w_bf, x2, l):
    n = u.shape[0]
    tm = _pick(n, 512)
    return pl.pallas_call(
        _outproj_kernel,
        grid=(n // tm,),
        in_specs=[pl.BlockSpec((tm, D_MODEL), lambda i: (i, 0)),
                  pl.BlockSpec((None, D_MODEL, D_MODEL), lambda i: (l, 0, 0)),
                  pl.BlockSpec((tm, D_MODEL), lambda i: (i, 0))],
        out_specs=pl.BlockSpec((tm, D_MODEL), lambda i: (i, 0)),
        out_shape=jax.ShapeDtypeStruct((n, D_MODEL), F32),
        compiler_params=_cparams(("parallel",), 48),
    )(u, w_bf, x2)


ROUTE_GROUP_LANE = N_EXPERTS


def _router_kernel(x_ref, nw_ref, wcat_ref, br_ref, o_ref, ot_ref, cnt_ref, run_scr):
    @pl.when(pl.program_id(0) == 0)
    def _():
        run_scr[...] = jnp.zeros_like(run_scr)

    tm = x_ref.shape[0]
    xn = _rms(x_ref[...], nw_ref[...])
    hi = xn.astype(BF16)
    lo = (xn - hi.astype(F32)).astype(BF16)
    both = jnp.dot(hi, wcat_ref[...], preferred_element_type=F32)
    logit = (both[:, :LANES] + both[:, LANES:]
             + jnp.dot(lo, wcat_ref[:, :LANES], preferred_element_type=F32) + br_ref[...])
    lane = lax.broadcasted_iota(jnp.int32, logit.shape, 1).astype(F32)
    big = 1e9
    is_g = (lane >= ROUTE_GROUP_LANE) & (lane < ROUTE_GROUP_LANE + N_GROUPS)
    lg = jnp.where(is_g, logit, NEG_BIG)
    mg = jnp.max(lg, axis=-1, keepdims=True)
    g_sel = jnp.min(jnp.where(lg == mg, lane - ROUTE_GROUP_LANE, big), axis=-1, keepdims=True)
    p_group = 1.0 / jnp.sum(jnp.exp(lg - mg), axis=-1, keepdims=True)
    lo_lane = g_sel * EXPERTS_PER_GROUP
    in_g = (lane >= lo_lane) & (lane < lo_lane + EXPERTS_PER_GROUP)
    le = jnp.where(in_g, logit, NEG_BIG)
    v1 = jnp.max(le, axis=-1, keepdims=True)
    i1 = jnp.min(jnp.where(le == v1, lane, big), axis=-1, keepdims=True)
    le2 = jnp.where(lane == i1, NEG_BIG, le)
    v2 = jnp.max(le2, axis=-1, keepdims=True)
    i2 = jnp.min(jnp.where(le2 == v2, lane, big), axis=-1, keepdims=True)
    e2 = jnp.exp(v2 - v1)
    den = 1.0 + e2
    p1 = p_group / den
    p2 = p_group * e2 / den
    oh1 = jnp.where(lane == i1, 1.0, 0.0)
    oh2 = jnp.where(lane == i2, 1.0, 0.0)
    oh = oh1 + oh2
    row = lax.broadcasted_iota(jnp.int32, (tm, tm), 0)
    col = lax.broadcasted_iota(jnp.int32, (tm, tm), 1)
    tri = jnp.where(row > col, 1.0, 0.0).astype(BF16)
    before = run_scr[...] + jnp.dot(tri, oh.astype(BF16), preferred_element_type=F32)
    r1 = jnp.sum(oh1 * before, axis=-1, keepdims=True)
    r2 = jnp.sum(oh2 * before, axis=-1, keepdims=True)
    run = run_scr[...] + jnp.sum(oh, axis=0, keepdims=True)
    run_scr[...] = run
    cnt_ref[...] = jnp.broadcast_to(run, cnt_ref.shape)
    out = jnp.zeros_like(logit)
    for n, val in enumerate((i1, i2, p1, p2, r1, r2)):
        out = jnp.where(lane == n, val, out)
    o_ref[...] = out
    ot_ref[...] = jnp.transpose(out)[:8]


def _router(x2, nw, w_rg, b_rg, w_re, b_re):
    n = x2.shape[0]
    tm = _pick(n, 512)
    w = jnp.zeros((D_MODEL, LANES), F32).at[:, :N_EXPERTS].set(w_re).at[:, N_EXPERTS:N_EXPERTS + N_GROUPS].set(w_rg)
    br = jnp.zeros((1, LANES), F32).at[0, :N_EXPERTS].set(b_re).at[0, N_EXPERTS:N_EXPERTS + N_GROUPS].set(b_rg)
    whi = w.astype(BF16)
    wlo = (w - whi.astype(F32)).astype(BF16)
    full = lambda shape: pl.BlockSpec(shape, lambda i: (0, 0))
    return pl.pallas_call(
        _router_kernel,
        grid=(n // tm,),
        in_specs=[pl.BlockSpec((tm, D_MODEL), lambda i: (i, 0)), full((1, D_MODEL)),
                  full((D_MODEL, 2 * LANES)), full((1, LANES))],
        out_specs=[pl.BlockSpec((tm, LANES), lambda i: (i, 0)), pl.BlockSpec((8, tm), lambda i: (0, i)),
                   full((8, LANES))],
        out_shape=[jax.ShapeDtypeStruct((n, LANES), F32), jax.ShapeDtypeStruct((8, n), F32),
                   jax.ShapeDtypeStruct((8, LANES), F32)],
        scratch_shapes=[pltpu.VMEM((1, LANES), F32)],
        compiler_params=_cparams(("arbitrary",), 32),
    )(x2, nw.reshape(1, D_MODEL), jnp.concatenate([whi, wlo], axis=1), br)


def _moe_plan(e1, e2, r1, r2, cnt, tm_e):
    n = e1.shape[0]
    ar = jnp.arange(N_EXPERTS, dtype=jnp.int32)
    offs = jnp.cumsum(cnt) - cnt
    dest = jnp.stack([jnp.take(offs, e1) + r1, jnp.take(offs, e2) + r2])
    n_tiles = (2 * n) // tm_e
    first = offs // tm_e
    last = jnp.where(cnt > 0, (offs + cnt - 1) // tm_e, first - 1)
    n_items = last - first + 1
    item_end = jnp.cumsum(n_items)
    item_start = item_end - n_items
    n_work = n_tiles + N_EXPERTS - 1
    w = jnp.arange(n_work, dtype=jnp.int32)
    e_w = jnp.minimum(jnp.sum((w[:, None] >= item_end[None, :]).astype(jnp.int32), axis=1), N_EXPERTS - 1)
    valid = w < item_end[-1]
    ohw = (e_w[:, None] == ar[None, :]).astype(jnp.int32)
    sel = lambda tab: jnp.sum(ohw * tab[None, :], axis=1)
    tile = sel(first) + (w - sel(item_start))
    lo = jnp.maximum(sel(offs), tile * tm_e)
    hi = jnp.minimum(sel(offs) + sel(cnt), (tile + 1) * tm_e)
    last_e = jnp.max(jnp.where(n_items > 0, ar, 0))
    tile = jnp.where(valid, tile, n_tiles - 1)
    e_w = jnp.where(valid, e_w, last_e)
    lo = jnp.where(valid, lo, 0)
    hi = jnp.where(valid, hi, 0)
    return dest.astype(jnp.int32), tile.astype(jnp.int32), e_w.astype(jnp.int32), lo.astype(jnp.int32), hi.astype(jnp.int32)


def _row_copy(src_ref, src_row, dst_ref, dst_row, sem):
    return pltpu.make_async_copy(src_ref.at[pl.ds(src_row, 1), :], dst_ref.at[pl.ds(dst_row, 1), :], sem)


ROW_UNROLL = 8


def _dispatch_kernel(dest_ref, x_ref, xs_ref, sem, *, tm):
    def copies(i):
        for u in range(ROW_UNROLL):
            r = i * ROW_UNROLL + u
            for k in range(2):
                yield _row_copy(x_ref, r, xs_ref, dest_ref[k, r], sem)

    def issue(i, carry):
        for cp in copies(i):
            cp.start()
        return carry

    def drain(i, carry):
        for cp in copies(i):
            cp.wait()
        return carry

    lax.fori_loop(0, tm // ROW_UNROLL, issue, 0)
    lax.fori_loop(0, tm // ROW_UNROLL, drain, 0)


def _dispatch(x2, dest3, tm):
    n = x2.shape[0]
    return pl.pallas_call(
        functools.partial(_dispatch_kernel, tm=tm),
        grid=(n // tm,),
        in_specs=[pl.BlockSpec((None, 2, tm), lambda i: (i, 0, 0), memory_space=pltpu.SMEM),
                  pl.BlockSpec((tm, D_MODEL), lambda i: (i, 0))],
        out_specs=pl.BlockSpec(memory_space=pl.ANY),
        out_shape=jax.ShapeDtypeStruct((2 * n, D_MODEL), F32),
        scratch_shapes=[pltpu.SemaphoreType.DMA(())],
        compiler_params=_cparams(("arbitrary",), 32),
    )(dest3, x2)


def _experts_kernel(tile_ref, exp_ref, lo_ref, hi_ref, xs_ref, nw_ref, wg_ref, wu_ref, wd_ref, ys_ref, *, tm):
    w = pl.program_id(0)
    lo = lo_ref[w]
    hi = hi_ref[w]
    tile = tile_ref[w]
    first_visit = jnp.logical_or(w == 0, tile_ref[jnp.maximum(w - 1, 0)] != tile)

    @pl.when(hi > lo)
    def _():
        xn = _rms(xs_ref[...], nw_ref[...]).astype(BF16)
        gate = jnp.dot(xn, wg_ref[...], preferred_element_type=F32)
        up = jnp.dot(xn, wu_ref[...], preferred_element_type=F32)
        h = (gate * jax.nn.sigmoid(gate) * up).astype(BF16)
        y = jnp.dot(h, wd_ref[...], preferred_element_type=F32)

        @pl.when(first_visit)
        def _():
            ys_ref[...] = y

        @pl.when(jnp.logical_not(first_visit))
        def _():
            rows = tile * tm + lax.broadcasted_iota(jnp.int32, (tm, 1), 0)
            ys_ref[...] = jnp.where((rows >= lo) & (rows < hi), y, ys_ref[...])


def _experts(xs, nw, wg_bf, wu_bf, wd_bf, l, tile, exp, lo, hi, tm):
    n_work = tile.shape[0]
    grid_spec = pltpu.PrefetchScalarGridSpec(
        num_scalar_prefetch=4,
        grid=(n_work,),
        in_specs=[pl.BlockSpec((tm, D_MODEL), lambda w, t, e, lo, hi: (t[w], 0)),
                  pl.BlockSpec((1, D_MODEL), lambda w, t, e, lo, hi: (0, 0)),
                  pl.BlockSpec((None, None, D_MODEL, D_EXPERT), lambda w, t, e, lo, hi: (l, e[w], 0, 0)),
                  pl.BlockSpec((None, None, D_MODEL, D_EXPERT), lambda w, t, e, lo, hi: (l, e[w], 0, 0)),
                  pl.BlockSpec((None, None, D_EXPERT, D_MODEL), lambda w, t, e, lo, hi: (l, e[w], 0, 0))],
        out_specs=pl.BlockSpec((tm, D_MODEL), lambda w, t, e, lo, hi: (t[w], 0)),
    )
    return pl.pallas_call(
        functools.partial(_experts_kernel, tm=tm),
        grid_spec=grid_spec,
        out_shape=jax.ShapeDtypeStruct(xs.shape, F32),
        compiler_params=_cparams(("arbitrary",), 48),
    )(tile, exp, lo, hi, xs, nw.reshape(1, D_MODEL), wg_bf, wu_bf, wd_bf)


def _combine_kernel(dest_ref, route_ref, x_ref, ys_ref, o_ref, buf0, buf1, sem, *, tm):
    def copies(i):
        for u in range(ROW_UNROLL):
            r = i * ROW_UNROLL + u
            yield _row_copy(ys_ref, dest_ref[0, r], buf0, r, sem)
            yield _row_copy(ys_ref, dest_ref[1, r], buf1, r, sem)

    def issue(i, carry):
        for cp in copies(i):
            cp.start()
        return carry

    def drain(i, carry):
        for cp in copies(i):
            cp.wait()
        return carry

    lax.fori_loop(0, tm // ROW_UNROLL, issue, 0)
    lax.fori_loop(0, tm // ROW_UNROLL, drain, 0)
    route = route_ref[...]
    o_ref[...] = x_ref[...] + route[:, 2:3] * buf0[...] + route[:, 3:4] * buf1[...]


def _combine(x2, route, ys, dest3, tm):
    n = x2.shape[0]
    return pl.pallas_call(
        functools.partial(_combine_kernel, tm=tm),
        grid=(n // tm,),
        in_specs=[pl.BlockSpec((None, 2, tm), lambda i: (i, 0, 0), memory_space=pltpu.SMEM),
                  pl.BlockSpec((tm, LANES), lambda i: (i, 0)),
                  pl.BlockSpec((tm, D_MODEL), lambda i: (i, 0)),
                  pl.BlockSpec(memory_space=pl.ANY)],
        out_specs=pl.BlockSpec((tm, D_MODEL), lambda i: (i, 0)),
        out_shape=jax.ShapeDtypeStruct((n, D_MODEL), F32),
        scratch_shapes=[pltpu.VMEM((tm, D_MODEL), F32), pltpu.VMEM((tm, D_MODEL), F32),
                        pltpu.SemaphoreType.DMA(())],
        compiler_params=_cparams(("arbitrary",), 32),
    )(dest3, route, x2, ys)


def _ffn(x2, lw, l):
    n = x2.shape[0]
    route, route_t, cnt = _router(x2, lw["norm_ffn_w"], lw["w_rg"], lw["b_rg"], lw["w_re"], lw["b_re"])
    fields = route_t.astype(jnp.int32)
    tm_e = _pick(2 * n, 512) if n >= 4096 else 128
    tm_r = _pick(n, 256)
    dest, tile, exp, lo, hi = _moe_plan(fields[0], fields[1], fields[4], fields[5],
                                        cnt[0, :N_EXPERTS].astype(jnp.int32), tm_e)
    dest3 = dest.reshape(2, n // tm_r, tm_r).transpose(1, 0, 2)
    xs = _dispatch(x2, dest3, tm_r)
    ys = _experts(xs, lw["norm_ffn_w"], lw["wg"], lw["wu"], lw["wd"], l, tile, exp, lo, hi, tm_e)
    return _combine(x2, route, ys, dest3, tm_r)


def _layer(x3, pos, l, depth, lw, ret_state, kv_cache, prev_kv):
    bsz, seq, _ = x3.shape
    n = bsz * seq
    sample = ret_state is not None
    lam_init = 0.8 - 0.6 * math.exp(-0.3 * l)
    x2 = x3.reshape(n, D_MODEL)
    proj2 = _inproj(x2, lw["norm_mix_w"], lw["w_in"], l)
    proj3 = proj2.reshape(bsz, seq, D_IN)

    a_y, a_v = _gmlp(proj2, seq, lw["a_norm_w"], lw["a_ws"], lw["a_bs_b"], emit_v=sample)
    b_y, s_new = _retention(proj3, pos, lw["b_norm_w"], ret_state)
    prep = _cprep(proj3, lw["c_qnorm_w"], lw["c_knorm_w"], lw["c_lambda"], lam_init, not sample, l, depth, prev_kv)
    qn, kn, kf_all, vf_all, lam_b = prep[:5]
    if sample:
        c_y = _dattn_sample(qn, kn, proj3, kv_cache[0], kv_cache[1], l, lam_b, lw["c_subln_w"], 1.0 - lam_init)
    else:
        c_y = _dattn_prompt(qn, kn, prep[5], lam_b, lw["c_subln_w"], 1.0 - lam_init)

    u = _merge(a_y, b_y.reshape(n, A_WIDTH), c_y.reshape(n, A_WIDTH), proj2, lw["w_branch"], l)
    x2 = _outproj(u, lw["w_out"], x2, l)
    x2 = _ffn(x2, lw, l)
    return x2.reshape(bsz, seq, D_MODEL), (kf_all, vf_all), s_new, a_v


def kernel(x_prompt, x_sample, cache_k_c, cache_v_c, state_ret, norm_mix_w, w_in, a_norm_w, a_ws, a_bs, b_norm_w, c_qnorm_w, c_knorm_w, c_lambda, c_subln_w, w_branch, w_out, norm_ffn_w, w_router_group, b_router_group, w_router_expert, b_router_expert, w_gate_e, w_up_e, w_down_e):
    depth = w_in.shape[0]
    past = cache_k_c.shape[2]
    pos_p = jnp.arange(x_prompt.shape[1])
    pos_s = past + jnp.arange(x_sample.shape[1])
    yp, ys = x_prompt, x_sample
    big = dict(w_in=w_in.astype(BF16), w_branch=w_branch.astype(BF16), w_out=w_out.astype(BF16),
               wg=w_gate_e.astype(BF16), wu=w_up_e.astype(BF16), wd=w_down_e.astype(BF16))
    kv_p = kv_s = None
    rets_p, rets_s, avs = [], [], []
    for l in range(depth):
        lw = dict(
            big, norm_mix_w=norm_mix_w[l], a_norm_w=a_norm_w[l], a_ws=a_ws[l],
            a_bs_b=jnp.broadcast_to(a_bs[l][:, :, None], (A_GROUPS, A_BLOCK, A_GROUP)),
            b_norm_w=b_norm_w[l], c_qnorm_w=c_qnorm_w[l], c_knorm_w=c_knorm_w[l], c_lambda=c_lambda[l],
            c_subln_w=c_subln_w[l], norm_ffn_w=norm_ffn_w[l], w_rg=w_router_group[l], b_rg=b_router_group[l],
            w_re=w_router_expert[l], b_re=b_router_expert[l])
        yp, kv_p, rp, _ = _layer(yp, pos_p, l, depth, lw, None, None, kv_p)
        ys, kv_s, rn, avn = _layer(ys, pos_s, l, depth, lw, state_ret[l], (cache_k_c, cache_v_c), kv_s)
        rets_p.append(rp)
        rets_s.append(rn)
        avs.append(avn.reshape(ys.shape[0], ys.shape[1], A_WIDTH))
    as_cache = lambda buf, x: buf.reshape(depth, x.shape[0], x.shape[1], C_HEADS, LANES)
    return (yp, ys, as_cache(kv_p[0], x_prompt), as_cache(kv_p[1], x_prompt), jnp.stack(rets_p, 0),
            as_cache(kv_s[0], x_sample), as_cache(kv_s[1], x_sample), jnp.stack(rets_s, 0), jnp.stack(avs, 0))
```

```python
import functools
import math

import jax
import jax.numpy as jnp
from jax import lax
from jax.experimental import pallas as pl
from jax.experimental.pallas import tpu as pltpu

F32 = jnp.float32
BF16 = jnp.bfloat16

D_MODEL = 2048
CHUNK = 64
A_WIDTH = 1024
A_BLOCK = 128
A_GROUP = 128
A_GROUPS = 8
B_HEADS = 4
B_DK = 128
B_DV = 256
ROPE_BASE = 10000.0
C_HEADS = 8
C_DH = 64
C_DV = 128
N_GROUPS = 4
EXPERTS_PER_GROUP = 4
N_EXPERTS = 16
D_EXPERT = 512
EPS = 1e-6
D_IN = 14336

OFF_AU, OFF_AV, OFF_BQ, OFF_BK, OFF_BV, OFF_BG, OFF_CQ, OFF_CK, OFF_CV, OFF_GATE = (
    0, 1024, 2048, 2560, 3072, 4096, 5120, 6144, 7168, 8192)

LANES = 128
MIB = 1024 * 1024
NEG_BIG = -1e30
Q_SCALE = (C_DH ** -0.5) * math.log2(math.e)
VT_ROWS = C_DV + 16
HEADS_PER_STEP = 2


def _cparams(sem, vmem_mib):
    return pltpu.CompilerParams(dimension_semantics=sem, vmem_limit_bytes=vmem_mib * MIB)


def _rms(x, w):
    ms = jnp.mean(x * x, axis=-1, keepdims=True)
    return x * lax.rsqrt(ms + EPS) * w


def _pick(n, pref):
    t = min(pref, n)
    while n % t:
        t //= 2
    return t


def _inproj_kernel(x_ref, g_ref, w_ref, o_ref, xn_ref):
    @pl.when(pl.program_id(1) == 0)
    def _():
        xn_ref[...] = _rms(x_ref[...], g_ref[...]).astype(BF16)

    o_ref[...] = jnp.dot(xn_ref[...], w_ref[...], preferred_element_type=F32).astype(o_ref.dtype)


def _inproj(x2, g, w_bf, l):
    n = x2.shape[0]
    tm = _pick(n, 1024)
    tn = 1024
    return pl.pallas_call(
        _inproj_kernel,
        grid=(n // tm, D_IN // tn),
        in_specs=[pl.BlockSpec((tm, D_MODEL), lambda i, j: (i, 0)),
                  pl.BlockSpec((1, D_MODEL), lambda i, j: (0, 0)),
                  pl.BlockSpec((None, D_MODEL, tn), lambda i, j: (l, 0, j))],
        out_specs=pl.BlockSpec((tm, tn), lambda i, j: (i, j)),
        out_shape=jax.ShapeDtypeStruct((n, D_IN), BF16),
        scratch_shapes=[pltpu.VMEM((tm, D_MODEL), BF16)],
        compiler_params=_cparams(("parallel", "arbitrary"), 48),
    )(x2, g.reshape(1, D_MODEL), w_bf)


def _gmlp_kernel(u_ref, v_ref, nw_ref, ws_ref, bs_ref, y_ref, *rest, pb, nblk, emit_v):
    u = jax.nn.gelu(u_ref[...].astype(F32))
    v = _rms(jax.nn.gelu(v_ref[...].astype(F32)), nw_ref[...])
    if emit_v:
        rest[0][...] = v
    vb = v.astype(BF16)
    p = lax.broadcasted_iota(jnp.int32, (A_BLOCK, A_BLOCK), 0)
    q = lax.broadcasted_iota(jnp.int32, (A_BLOCK, A_BLOCK), 1)
    mask = (q // CHUNK) <= (p // CHUNK)
    for g in range(A_GROUPS):
        w = jnp.where(mask, ws_ref[g], 0.0).astype(BF16)
        bias = bs_ref[g, :pb, :]
        cs = slice(g * A_GROUP, (g + 1) * A_GROUP)
        for b in range(nblk):
            rs = slice(b * pb, (b + 1) * pb)
            vg = vb[rs, cs]
            if pb < A_BLOCK:
                vg = jnp.concatenate([vg, jnp.zeros((A_BLOCK - pb, A_GROUP), BF16)], axis=0)
            s = jnp.dot(w, vg, preferred_element_type=F32)[:pb] + bias
            y_ref[rs, cs] = (u[rs, cs] * s).astype(y_ref.dtype)


def _gmlp(proj2, seq_len, nw, ws, bs_b, emit_v):
    n = proj2.shape[0]
    if seq_len % A_BLOCK == 0:
        pb, nblk = A_BLOCK, 4 if seq_len % (4 * A_BLOCK) == 0 else 1
    else:
        assert seq_len <= A_BLOCK
        pb, nblk = seq_len, 1
    r = pb * nblk
    out_shape = [jax.ShapeDtypeStruct((n, A_WIDTH), BF16)]
    out_specs = [pl.BlockSpec((r, A_WIDTH), lambda i: (i, 0))]
    if emit_v:
        out_shape.append(jax.ShapeDtypeStruct((n, A_WIDTH), F32))
        out_specs.append(pl.BlockSpec((r, A_WIDTH), lambda i: (i, 0)))
    res = pl.pallas_call(
        functools.partial(_gmlp_kernel, pb=pb, nblk=nblk, emit_v=emit_v),
        grid=(n // r,),
        in_specs=[pl.BlockSpec((r, A_WIDTH), lambda i: (i, OFF_AU // A_WIDTH)),
                  pl.BlockSpec((r, A_WIDTH), lambda i: (i, OFF_AV // A_WIDTH)),
                  pl.BlockSpec((1, A_WIDTH), lambda i: (0, 0)),
                  pl.BlockSpec((A_GROUPS, A_BLOCK, A_BLOCK), lambda i: (0, 0, 0)),
                  pl.BlockSpec((A_GROUPS, A_BLOCK, A_GROUP), lambda i: (0, 0, 0))],
        out_specs=out_specs,
        out_shape=out_shape,
        compiler_params=_cparams(("parallel",), 32),
    )(proj2, proj2, nw.reshape(1, A_WIDTH), ws, bs_b)
    return res if emit_v else (res[0], None)


def _ret_kernel(q_ref, k_ref, v_ref, g_ref, cos_ref, sin_ref, dm_ref, qd_ref, kd_ref, sd_ref, nw_ref, *rest,
                has_init):
    if has_init:
        s0_ref, y_ref, sfin_ref, s_scr = rest
    else:
        y_ref, sfin_ref, s_scr = rest
    c = pl.program_id(1)

    @pl.when(c == 0)
    def _():
        s_scr[...] = s0_ref[...] if has_init else jnp.zeros_like(s_scr)

    cos = cos_ref[...]
    sin = sin_ref[...]
    nw = nw_ref[...]
    for h in range(B_HEADS):
        ks = slice(h * B_DK, (h + 1) * B_DK)
        vs = slice(h * B_DV, (h + 1) * B_DV)
        q = q_ref[:, ks].astype(F32)
        k = k_ref[:, ks].astype(F32)
        qr = q * cos + pltpu.roll(q, B_DK // 2, 1) * sin
        kr = (k * cos + pltpu.roll(k, B_DK // 2, 1) * sin) * (B_DK ** -0.5)
        qb = qr.astype(BF16)
        kb = kr.astype(BF16)
        v = v_ref[:, vs]
        attn = lax.dot_general(qb, kb, (((1,), (1,)), ((), ())), preferred_element_type=F32) * dm_ref[h]
        intra = jnp.dot(attn.astype(BF16), v, preferred_element_type=F32)
        s_prev = s_scr[h]
        inter = jnp.dot(qb, s_prev.astype(BF16), preferred_element_type=F32) * qd_ref[h]
        o = intra + inter
        kt = jnp.transpose(kr * kd_ref[h]).astype(BF16)
        s_scr[h] = sd_ref[h] * s_prev + jnp.dot(kt, v, preferred_element_type=F32)
        gate = g_ref[:, vs].astype(F32)
        y_ref[:, vs] = (_rms(o, nw) * (gate * jax.nn.sigmoid(gate))).astype(y_ref.dtype)

    @pl.when(c == pl.num_programs(1) - 1)
    def _():
        sfin_ref[...] = s_scr[...]


def _ret_tables(c):
    lg = jnp.log1p(-jnp.exp2(-5.0 - jnp.arange(B_HEADS, dtype=F32)))
    idx = jnp.arange(c, dtype=F32)
    diff = idx[:, None] - idx[None, :]
    dm = jnp.where(diff >= 0, jnp.exp(jnp.maximum(diff, 0.0)[None] * lg[:, None, None]), 0.0)
    qd = jnp.exp((idx + 1.0)[None, :] * lg[:, None])
    kd = jnp.exp((c - 1.0 - idx)[None, :] * lg[:, None])
    sd = jnp.exp(c * lg)
    return (dm, jnp.broadcast_to(qd[:, :, None], (B_HEADS, c, B_DV)),
            jnp.broadcast_to(kd[:, :, None], (B_HEADS, c, B_DK)),
            jnp.broadcast_to(sd[:, None, None], (B_HEADS, 1, B_DV)))


def _rope_tables(pos):
    half = B_DK // 2
    inv_freq = 1.0 / (ROPE_BASE ** jnp.linspace(0.0, 1.0, half, dtype=F32))
    ang = pos.astype(F32)[:, None] * inv_freq[None, :]
    cos, sin = jnp.cos(ang), jnp.sin(ang)
    return jnp.concatenate([cos, cos], axis=-1), jnp.concatenate([-sin, sin], axis=-1)


def _retention(proj3, pos, nw, s0):
    bsz, seq, _ = proj3.shape
    c = _pick(seq, 256)
    dm, qd, kd, sd = _ret_tables(c)
    cos, sin = _rope_tables(pos)
    has_init = s0 is not None
    qk_w = B_HEADS * B_DK
    v_w = B_HEADS * B_DV
    full = lambda shape: pl.BlockSpec(shape, lambda b, i: (0,) * len(shape))
    state = pl.BlockSpec((None, B_HEADS, B_DK, B_DV), lambda b, i: (b, 0, 0, 0))
    in_specs = [pl.BlockSpec((None, c, qk_w), lambda b, i: (b, i, OFF_BQ // qk_w)),
                pl.BlockSpec((None, c, qk_w), lambda b, i: (b, i, OFF_BK // qk_w)),
                pl.BlockSpec((None, c, v_w), lambda b, i: (b, i, OFF_BV // v_w)),
                pl.BlockSpec((None, c, v_w), lambda b, i: (b, i, OFF_BG // v_w)),
                pl.BlockSpec((c, B_DK), lambda b, i: (i, 0)),
                pl.BlockSpec((c, B_DK), lambda b, i: (i, 0)),
                full((B_HEADS, c, c)), full((B_HEADS, c, B_DV)), full((B_HEADS, c, B_DK)),
                full((B_HEADS, 1, B_DV)), full((1, B_DV))]
    args = [proj3, proj3, proj3, proj3, cos, sin, dm, qd, kd, sd, nw.reshape(1, B_DV)]
    if has_init:
        in_specs.append(state)
        args.append(s0)
    return pl.pallas_call(
        functools.partial(_ret_kernel, has_init=has_init),
        grid=(bsz, seq // c),
        in_specs=in_specs,
        out_specs=[pl.BlockSpec((None, c, v_w), lambda b, i: (b, i, 0)), state],
        out_shape=[jax.ShapeDtypeStruct((bsz, seq, v_w), BF16),
                   jax.ShapeDtypeStruct((bsz, B_HEADS, B_DK, B_DV), F32)],
        scratch_shapes=[pltpu.VMEM((B_HEADS, B_DK, B_DV), F32)],
        compiler_params=_cparams(("parallel", "arbitrary"), 32),
    )(*args)


def _group_mean_matrix():
    r = lax.broadcasted_iota(jnp.int32, (LANES, LANES), 0)
    c = lax.broadcasted_iota(jnp.int32, (LANES, LANES), 1)
    return jnp.where((r // C_DH) == (c // C_DH), 1.0 / C_DH, 0.0).astype(BF16)


def _cprep_kernel(q_ref, k_ref, v_ref, qw_ref, kw_ref, lam_ref, *rest, lam_init, emit_vt, n_prev, l):
    qn_ref, kn_ref, kf_ref, vf_ref, lamo_ref = rest[n_prev:n_prev + 5]
    rows = q_ref.shape[0]
    if n_prev == 0:
        for d in range(kf_ref.shape[0]):
            if d != l:
                kf_ref[d] = jnp.zeros(kf_ref.shape[1:], F32)
                vf_ref[d] = jnp.zeros(vf_ref.shape[1:], F32)
        kf_ref, vf_ref = kf_ref.at[l], vf_ref.at[l]
    gm = _group_mean_matrix()
    qw = qw_ref[...]
    kw = kw_ref[...]
    for h in range(C_HEADS):
        cs = slice(h * LANES, (h + 1) * LANES)
        hs = pl.ds(h, rows, stride=C_HEADS)
        x = q_ref[:, cs].astype(F32)
        ms = jnp.dot((x * x).astype(BF16), gm, preferred_element_type=F32)
        qn_ref[:, cs] = (x * lax.rsqrt(ms + EPS) * qw * Q_SCALE).astype(BF16)
        x = k_ref[:, cs].astype(F32)
        ms = jnp.dot((x * x).astype(BF16), gm, preferred_element_type=F32)
        kn = x * lax.rsqrt(ms + EPS) * kw
        kf_ref[hs, :] = kn
        kn_ref[:, cs] = kn.astype(BF16)
        vv = v_ref[:, cs].astype(F32)
        vf_ref[hs, :] = vv
        if emit_vt:
            vt_ref = rest[n_prev + 5]
            vt_ref[h, :C_DV, :] = jnp.transpose(vv).astype(BF16)
            vt_ref[h, C_DV:, :] = jnp.ones((VT_ROWS - C_DV, rows), BF16)
    lq = lam_ref[...]
    l01 = jnp.sum(lq[0:1] * lq[1:2], axis=-1, keepdims=True)
    l23 = jnp.sum(lq[2:3] * lq[3:4], axis=-1, keepdims=True)
    lam = jnp.exp(l01) - jnp.exp(l23) + lam_init
    lamo_ref[...] = jnp.broadcast_to(lam, lamo_ref.shape)


def _cprep(proj3, qw, kw, lam_p, lam_init, emit_vt, l, depth, prev_kv):
    bsz, seq, _ = proj3.shape
    r = _pick(seq, 512)
    w = C_HEADS * LANES
    qw2 = jnp.concatenate([qw, qw]).reshape(1, LANES)
    kw2 = jnp.concatenate([kw, kw]).reshape(1, LANES)
    blk = lambda off: pl.BlockSpec((None, r, w), lambda b, i: (b, i, off // w))
    row = pl.BlockSpec((None, r, w), lambda b, i: (b, i, 0))
    if prev_kv is None:
        kv = pl.BlockSpec((depth, None, r * C_HEADS, LANES), lambda b, i: (0, b, i, 0))
    else:
        kv = pl.BlockSpec((None, None, r * C_HEADS, LANES), lambda b, i: (l, b, i, 0))
    kv_shape = jax.ShapeDtypeStruct((depth, bsz, seq * C_HEADS, LANES), F32)
    out_shape = [jax.ShapeDtypeStruct((bsz, seq, w), BF16), jax.ShapeDtypeStruct((bsz, seq, w), BF16),
                 kv_shape, kv_shape, jax.ShapeDtypeStruct((8, LANES), F32)]
    out_specs = [row, row, kv, kv, pl.BlockSpec((8, LANES), lambda b, i: (0, 0))]
    if emit_vt:
        out_shape.append(jax.ShapeDtypeStruct((bsz, C_HEADS, VT_ROWS, seq), BF16))
        out_specs.append(pl.BlockSpec((None, C_HEADS, VT_ROWS, r), lambda b, i: (b, 0, 0, i)))
    in_specs = [blk(OFF_CQ), blk(OFF_CK), blk(OFF_CV),
                pl.BlockSpec((1, LANES), lambda b, i: (0, 0)),
                pl.BlockSpec((1, LANES), lambda b, i: (0, 0)),
                pl.BlockSpec((4, C_DH), lambda b, i: (0, 0))]
    args = [proj3, proj3, proj3, qw2, kw2, lam_p]
    aliases = {}
    if prev_kv is not None:
        in_specs += [pl.BlockSpec(memory_space=pl.ANY)] * 2
        aliases = {len(args): 2, len(args) + 1: 3}
        args += list(prev_kv)
    return pl.pallas_call(
        functools.partial(_cprep_kernel, lam_init=lam_init, emit_vt=emit_vt, n_prev=len(aliases), l=l),
        grid=(bsz, seq // r),
        in_specs=in_specs,
        out_specs=out_specs,
        out_shape=out_shape,
        input_output_aliases=aliases,
        compiler_params=_cparams(("arbitrary", "arbitrary"), 40),
    )(*args)


def _dattn_kernel(q_ref, k_ref, vt_ref, lam_ref, sw_ref, o_ref, s_a, s_b, m_scr, a_scr, *, t, out_scale):
    qi = pl.program_id(2)
    lane = lax.broadcasted_iota(jnp.int32, (t, LANES), 1)
    qc = []
    for hh in range(HEADS_PER_STEP):
        q = q_ref[:, hh * LANES:(hh + 1) * LANES]
        qc += [jnp.where(lane < C_DH, q, jnp.zeros_like(q)), jnp.where(lane >= C_DH, q, jnp.zeros_like(q))]
    m_scr[...] = jnp.full(m_scr.shape, NEG_BIG, F32)
    a_scr[...] = jnp.zeros(a_scr.shape, F32)

    def scores(blk, s_ref):
        rows = pl.ds(pl.multiple_of(blk * t, t), t)
        for ch in range(2 * HEADS_PER_STEP):
            hh = ch // 2
            kb = k_ref[rows, hh * LANES:(hh + 1) * LANES]
            s_ref[ch] = lax.dot_general(kb, qc[ch], (((1,), (1,)), ((), ())), preferred_element_type=F32)

    def update(blk, s_ref, masked=False):
        cols = pl.ds(pl.multiple_of(blk * t, t), t)
        if masked:
            kpos = lax.broadcasted_iota(jnp.int32, (t, t), 0)
            qpos = lax.broadcasted_iota(jnp.int32, (t, t), 1)
            vis = (kpos // CHUNK) <= (qpos // CHUNK)
        for ch in range(2 * HEADS_PER_STEP):
            s = s_ref[ch]
            if masked:
                s = jnp.where(vis, s, NEG_BIG)
            m_old = m_scr[ch]
            m_new = jnp.maximum(m_old, jnp.max(s, axis=0, keepdims=True))
            alpha = jnp.exp2(m_old - m_new)
            p = jnp.exp2(s - m_new).astype(BF16)
            a_scr[ch] = alpha * a_scr[ch] + jnp.dot(vt_ref[ch // 2, :, cols], p, preferred_element_type=F32)
            m_scr[ch] = m_new

    scores(qi, s_a)

    @pl.when(qi == 0)
    def _():
        update(qi, s_a, masked=True)

    @pl.when(qi > 0)
    def _():
        scores(0, s_b)
        update(qi, s_a, masked=True)

        def pair(jj, carry):
            j = 2 * jj
            scores(j + 1, s_a)
            update(j, s_b)
            scores(j + 2, s_b)
            update(j + 1, s_a)
            return carry

        lax.fori_loop(0, (qi - 1) // 2, pair, 0)

        @pl.when(qi % 2 == 1)
        def _():
            update(qi - 1, s_b)

        @pl.when(qi % 2 == 0)
        def _():
            scores(qi - 1, s_a)
            update(qi - 2, s_b)
            update(qi - 1, s_a)

    lam = lam_ref[0:1, 0:1]
    for hh in range(HEADS_PER_STEP):
        c0, c1 = 2 * hh, 2 * hh + 1
        o_t = (a_scr[c0, :C_DV, :] / a_scr[c0, C_DV:C_DV + 1, :]
               - lam * (a_scr[c1, :C_DV, :] / a_scr[c1, C_DV:C_DV + 1, :]))
        ms = jnp.mean(o_t * o_t, axis=0, keepdims=True)
        o = jnp.transpose(o_t * lax.rsqrt(ms + EPS))
        o_ref[:, hh * C_DV:(hh + 1) * C_DV] = (o * sw_ref[...] * out_scale).astype(o_ref.dtype)


def _dattn_prompt(qn, kn, vt, lam_b, sw, out_scale):
    bsz, seq, w = qn.shape
    t = _pick(seq, 512)
    hw = HEADS_PER_STEP * LANES
    chains = 2 * HEADS_PER_STEP
    return pl.pallas_call(
        functools.partial(_dattn_kernel, t=t, out_scale=out_scale),
        grid=(bsz, C_HEADS // HEADS_PER_STEP, seq // t),
        in_specs=[pl.BlockSpec((None, t, hw), lambda b, h, i: (b, i, h)),
                  pl.BlockSpec((None, seq, hw), lambda b, h, i: (b, 0, h)),
                  pl.BlockSpec((None, HEADS_PER_STEP, VT_ROWS, seq), lambda b, h, i: (b, h, 0, 0)),
                  pl.BlockSpec((8, LANES), lambda b, h, i: (0, 0)),
                  pl.BlockSpec((1, C_DV), lambda b, h, i: (0, 0))],
        out_specs=pl.BlockSpec((None, t, hw), lambda b, h, i: (b, i, h)),
        out_shape=jax.ShapeDtypeStruct((bsz, seq, w), BF16),
        scratch_shapes=[pltpu.VMEM((chains, t, t), F32), pltpu.VMEM((chains, t, t), F32),
                        pltpu.VMEM((chains, 1, t), F32), pltpu.VMEM((chains, VT_ROWS, t), F32)],
        compiler_params=_cparams(("parallel", "parallel", "arbitrary"), 48),
    )(qn, kn, vt, lam_b, sw.reshape(1, C_DV))


def _dattn_sample_kernel(q_ref, kc_ref, vc_ref, kn_ref, vn_ref, lam_ref, sw_ref, o_ref, *, past, out_scale):
    lq = q_ref.shape[0]
    lane = lax.broadcasted_iota(jnp.int32, (lq, LANES), 1)
    qpos_c = past + lax.broadcasted_iota(jnp.int32, (lq, past), 0)
    kpos_c = lax.broadcasted_iota(jnp.int32, (lq, past), 1)
    vis_c = (kpos_c // CHUNK) <= (qpos_c // CHUNK)
    qpos_n = past + lax.broadcasted_iota(jnp.int32, (lq, lq), 0)
    kpos_n = past + lax.broadcasted_iota(jnp.int32, (lq, lq), 1)
    vis_n = (kpos_n // CHUNK) <= (qpos_n // CHUNK)
    lam = lam_ref[0:1, 0:1]
    nt = (((1,), (1,)), ((), ()))
    for h in range(C_HEADS):
        cs = slice(h * LANES, (h + 1) * LANES)
        q = q_ref[:, cs]
        kc = kc_ref[pl.ds(h, past, stride=C_HEADS), :].astype(BF16)
        vc = vc_ref[pl.ds(h, past, stride=C_HEADS), :].astype(BF16)
        kn = kn_ref[:, cs]
        vn = vn_ref[:, cs]
        outs = []
        for c in range(2):
            qc = jnp.where((lane // C_DH) == c, q, jnp.zeros_like(q))
            s_c = jnp.where(vis_c, lax.dot_general(qc, kc, nt, preferred_element_type=F32), NEG_BIG)
            s_n = jnp.where(vis_n, lax.dot_general(qc, kn, nt, preferred_element_type=F32), NEG_BIG)
            m = jnp.maximum(jnp.max(s_c, axis=-1, keepdims=True), jnp.max(s_n, axis=-1, keepdims=True))
            p_c = jnp.exp2(s_c - m)
            p_n = jnp.exp2(s_n - m)
            l = jnp.sum(p_c, axis=-1, keepdims=True) + jnp.sum(p_n, axis=-1, keepdims=True)
            acc = (jnp.dot(p_c.astype(BF16), vc, preferred_element_type=F32)
                   + jnp.dot(p_n.astype(BF16), vn, preferred_element_type=F32))
            outs.append(acc / l)
        o = outs[0] - lam * outs[1]
        o_ref[:, cs] = (_rms(o, sw_ref[...]) * out_scale).astype(o_ref.dtype)


def _dattn_sample(qn, kn, proj3, cache_k, cache_v, l, lam_b, sw, out_scale):
    bsz, lq, w = qn.shape
    depth, _, past = cache_k.shape[:3]
    ck = cache_k.reshape(depth, bsz, past * C_HEADS, LANES)
    cv = cache_v.reshape(depth, bsz, past * C_HEADS, LANES)
    rows = lambda r: pl.BlockSpec((None, r, w), lambda b: (b, 0, 0))
    cache = pl.BlockSpec((None, None, past * C_HEADS, LANES), lambda b: (l, b, 0, 0))
    return pl.pallas_call(
        functools.partial(_dattn_sample_kernel, past=past, out_scale=out_scale),
        grid=(bsz,),
        in_specs=[rows(lq), cache, cache, rows(lq),
                  pl.BlockSpec((None, lq, w), lambda b: (b, 0, OFF_CV // w)),
                  pl.BlockSpec((8, LANES), lambda b: (0, 0)),
                  pl.BlockSpec((1, C_DV), lambda b: (0, 0))],
        out_specs=rows(lq),
        out_shape=jax.ShapeDtypeStruct((bsz, lq, w), BF16),
        compiler_params=_cparams(("parallel",), 40),
    )(qn, ck, cv, kn, proj3, lam_b, sw.reshape(1, C_DV))


def _merge_kernel(ya_ref, yb_ref, yc_ref, ga_ref, gb_ref, gc_ref, w_ref, o_ref):
    acc = None
    for y_ref, g_ref, n in ((ya_ref, ga_ref, 0), (yb_ref, gb_ref, 1), (yc_ref, gc_ref, 2)):
        up = jnp.dot(y_ref[...], w_ref[n], preferred_element_type=F32)
        term = jax.nn.sigmoid(g_ref[...].astype(F32)) * up
        acc = term if acc is None else acc + term
    o_ref[...] = acc.astype(o_ref.dtype)


def _merge(ya, yb, yc, proj2, wb_bf, l):
    n = ya.shape[0]
    tm = _pick(n, 512)
    tn = 1024
    ybs = pl.BlockSpec((tm, A_WIDTH), lambda j, i: (i, 0))
    gate = lambda b: pl.BlockSpec((tm, tn), lambda j, i: (i, (OFF_GATE + b * D_MODEL) // tn + j))
    return pl.pallas_call(
        _merge_kernel,
        grid=(D_MODEL // tn, n // tm),
        in_specs=[ybs, ybs, ybs, gate(0), gate(1), gate(2),
                  pl.BlockSpec((None, 3, A_WIDTH, tn), lambda j, i: (l, 0, 0, j))],
        out_specs=pl.BlockSpec((tm, tn), lambda j, i: (i, j)),
        out_shape=jax.ShapeDtypeStruct((n, D_MODEL), BF16),
        compiler_params=_cparams(("parallel", "parallel"), 48),
    )(ya, yb, yc, proj2, proj2, proj2, wb_bf)


def _outproj_kernel(u_ref, w_ref, x_ref, o_ref):
    o_ref[...] = x_ref[...] + jnp.dot(u_ref[...], w_ref[...], preferred_element_type=F32)


def _outproj(u, w_bf, x2, l):
    n = u.shape[0]
    tm = _pick(n, 512)
    return pl.pallas_call(
        _outproj_kernel,
        grid=(n // tm,),
        in_specs=[pl.BlockSpec((tm, D_MODEL), lambda i: (i, 0)),
                  pl.BlockSpec((None, D_MODEL, D_MODEL), lambda i: (l, 0, 0)),
                  pl.BlockSpec((tm, D_MODEL), lambda i: (i, 0))],
        out_specs=pl.BlockSpec((tm, D_MODEL), lambda i: (i, 0)),
        out_shape=jax.ShapeDtypeStruct((n, D_MODEL), F32),
        compiler_params=_cparams(("parallel",), 48),
    )(u, w_bf, x2)


ROUTE_GROUP_LANE = N_EXPERTS


def _router_kernel(x_ref, nw_ref, wcat_ref, br_ref, o_ref, ot_ref, cnt_ref, run_scr):
    @pl.when(pl.program_id(0) == 0)
    def _():
        run_scr[...] = jnp.zeros_like(run_scr)

    tm = x_ref.shape[0]
    xn = _rms(x_ref[...], nw_ref[...])
    hi = xn.astype(BF16)
    lo = (xn - hi.astype(F32)).astype(BF16)
    both = jnp.dot(hi, wcat_ref[...], preferred_element_type=F32)
    logit = (both[:, :LANES] + both[:, LANES:]
             + jnp.dot(lo, wcat_ref[:, :LANES], preferred_element_type=F32) + br_ref[...])
    lane = lax.broadcasted_iota(jnp.int32, logit.shape, 1).astype(F32)
    big = 1e9
    is_g = (lane >= ROUTE_GROUP_LANE) & (lane < ROUTE_GROUP_LANE + N_GROUPS)
    lg = jnp.where(is_g, logit, NEG_BIG)
    mg = jnp.max(lg, axis=-1, keepdims=True)
    g_sel = jnp.min(jnp.where(lg == mg, lane - ROUTE_GROUP_LANE, big), axis=-1, keepdims=True)
    p_group = 1.0 / jnp.sum(jnp.exp(lg - mg), axis=-1, keepdims=True)
    lo_lane = g_sel * EXPERTS_PER_GROUP
    in_g = (lane >= lo_lane) & (lane < lo_lane + EXPERTS_PER_GROUP)
    le = jnp.where(in_g, logit, NEG_BIG)
    v1 = jnp.max(le, axis=-1, keepdims=True)
    i1 = jnp.min(jnp.where(le == v1, lane, big), axis=-1, keepdims=True)
    le2 = jnp.where(lane == i1, NEG_BIG, le)
    v2 = jnp.max(le2, axis=-1, keepdims=True)
    i2 = jnp.min(jnp.where(le2 == v2, lane, big), axis=-1, keepdims=True)
    e2 = jnp.exp(v2 - v1)
    den = 1.0 + e2
    p1 = p_group / den
    p2 = p_group * e2 / den
    oh1 = jnp.where(lane == i1, 1.0, 0.0)
    oh2 = jnp.where(lane == i2, 1.0, 0.0)
    oh = oh1 + oh2
    row = lax.broadcasted_iota(jnp.int32, (tm, tm), 0)
    col = lax.broadcasted_iota(jnp.int32, (tm, tm), 1)
    tri = jnp.where(row > col, 1.0, 0.0).astype(BF16)
    before = run_scr[...] + jnp.dot(tri, oh.astype(BF16), preferred_element_type=F32)
    r1 = jnp.sum(oh1 * before, axis=-1, keepdims=True)
    r2 = jnp.sum(oh2 * before, axis=-1, keepdims=True)
    run = run_scr[...] + jnp.sum(oh, axis=0, keepdims=True)
    run_scr[...] = run
    cnt_ref[...] = jnp.broadcast_to(run, cnt_ref.shape)
    out = jnp.zeros_like(logit)
    for n, val in enumerate((i1, i2, p1, p2, r1, r2)):
        out = jnp.where(lane == n, val, out)
    o_ref[...] = out
    ot_ref[...] = jnp.transpose(out)[:8]


def _router(x2, nw, w_rg, b_rg, w_re, b_re):
    n = x2.shape[0]
    tm = _pick(n, 512)
    w = jnp.zeros((D_MODEL, LANES), F32).at[:, :N_EXPERTS].set(w_re).at[:, N_EXPERTS:N_EXPERTS + N_GROUPS].set(w_rg)
    br = jnp.zeros((1, LANES), F32).at[0, :N_EXPERTS].set(b_re).at[0, N_EXPERTS:N_EXPERTS + N_GROUPS].set(b_rg)
    whi = w.astype(BF16)
    wlo = (w - whi.astype(F32)).astype(BF16)
    full = lambda shape: pl.BlockSpec(shape, lambda i: (0, 0))
    return pl.pallas_call(
        _router_kernel,
        grid=(n // tm,),
        in_specs=[pl.BlockSpec((tm, D_MODEL), lambda i: (i, 0)), full((1, D_MODEL)),
                  full((D_MODEL, 2 * LANES)), full((1, LANES))],
        out_specs=[pl.BlockSpec((tm, LANES), lambda i: (i, 0)), pl.BlockSpec((8, tm), lambda i: (0, i)),
                   full((8, LANES))],
        out_shape=[jax.ShapeDtypeStruct((n, LANES), F32), jax.ShapeDtypeStruct((8, n), F32),
                   jax.ShapeDtypeStruct((8, LANES), F32)],
        scratch_shapes=[pltpu.VMEM((1, LANES), F32)],
        compiler_params=_cparams(("arbitrary",), 32),
    )(x2, nw.reshape(1, D_MODEL), jnp.concatenate([whi, wlo], axis=1), br)


def _moe_plan(e1, e2, r1, r2, cnt, tm_e):
    n = e1.shape[0]
    ar = jnp.arange(N_EXPERTS, dtype=jnp.int32)
    offs = jnp.cumsum(cnt) - cnt
    dest = jnp.stack([jnp.take(offs, e1) + r1, jnp.take(offs, e2) + r2])
    n_tiles = (2 * n) // tm_e
    first = offs // tm_e
    last = jnp.where(cnt > 0, (offs + cnt - 1) // tm_e, first - 1)
    n_items = last - first + 1
    item_end = jnp.cumsum(n_items)
    item_start = item_end - n_items
    n_work = n_tiles + N_EXPERTS - 1
    w = jnp.arange(n_work, dtype=jnp.int32)
    e_w = jnp.minimum(jnp.sum((w[:, None] >= item_end[None, :]).astype(jnp.int32), axis=1), N_EXPERTS - 1)
    valid = w < item_end[-1]
    ohw = (e_w[:, None] == ar[None, :]).astype(jnp.int32)
    sel = lambda tab: jnp.sum(ohw * tab[None, :], axis=1)
    tile = sel(first) + (w - sel(item_start))
    lo = jnp.maximum(sel(offs), tile * tm_e)
    hi = jnp.minimum(sel(offs) + sel(cnt), (tile + 1) * tm_e)
    last_e = jnp.max(jnp.where(n_items > 0, ar, 0))
    tile = jnp.where(valid, tile, n_tiles - 1)
    e_w = jnp.where(valid, e_w, last_e)
    lo = jnp.where(valid, lo, 0)
    hi = jnp.where(valid, hi, 0)
    return dest.astype(jnp.int32), tile.astype(jnp.int32), e_w.astype(jnp.int32), lo.astype(jnp.int32), hi.astype(jnp.int32)


def _row_copy(src_ref, src_row, dst_ref, dst_row, sem):
    return pltpu.make_async_copy(src_ref.at[pl.ds(src_row, 1), :], dst_ref.at[pl.ds(dst_row, 1), :], sem)


ROW_UNROLL = 8


def _dispatch_kernel(dest_ref, x_ref, xs_ref, sem, *, tm):
    def copies(i):
        for u in range(ROW_UNROLL):
            r = i * ROW_UNROLL + u
            for k in range(2):
                yield _row_copy(x_ref, r, xs_ref, dest_ref[k, r], sem)

    def issue(i, carry):
        for cp in copies(i):
            cp.start()
        return carry

    def drain(i, carry):
        for cp in copies(i):
            cp.wait()
        return carry

    lax.fori_loop(0, tm // ROW_UNROLL, issue, 0)
    lax.fori_loop(0, tm // ROW_UNROLL, drain, 0)


def _dispatch(x2, dest3, tm):
    n = x2.shape[0]
    return pl.pallas_call(
        functools.partial(_dispatch_kernel, tm=tm),
        grid=(n // tm,),
        in_specs=[pl.BlockSpec((None, 2, tm), lambda i: (i, 0, 0), memory_space=pltpu.SMEM),
                  pl.BlockSpec((tm, D_MODEL), lambda i: (i, 0))],
        out_specs=pl.BlockSpec(memory_space=pl.ANY),
        out_shape=jax.ShapeDtypeStruct((2 * n, D_MODEL), F32),
        scratch_shapes=[pltpu.SemaphoreType.DMA(())],
        compiler_params=_cparams(("arbitrary",), 32),
    )(dest3, x2)


def _experts_kernel(tile_ref, exp_ref, lo_ref, hi_ref, xs_ref, nw_ref, wg_ref, wu_ref, wd_ref, ys_ref, *, tm):
    w = pl.program_id(0)
    lo = lo_ref[w]
    hi = hi_ref[w]
    tile = tile_ref[w]
    first_visit = jnp.logical_or(w == 0, tile_ref[jnp.maximum(w - 1, 0)] != tile)

    @pl.when(hi > lo)
    def _():
        xn = _rms(xs_ref[...], nw_ref[...]).astype(BF16)
        gate = jnp.dot(xn, wg_ref[...], preferred_element_type=F32)
        up = jnp.dot(xn, wu_ref[...], preferred_element_type=F32)
        h = (gate * jax.nn.sigmoid(gate) * up).astype(BF16)
        y = jnp.dot(h, wd_ref[...], preferred_element_type=F32)

        @pl.when(first_visit)
        def _():
            ys_ref[...] = y

        @pl.when(jnp.logical_not(first_visit))
        def _():
            rows = tile * tm + lax.broadcasted_iota(jnp.int32, (tm, 1), 0)
            ys_ref[...] = jnp.where((rows >= lo) & (rows < hi), y, ys_ref[...])


def _experts(xs, nw, wg_bf, wu_bf, wd_bf, l, tile, exp, lo, hi, tm):
    n_work = tile.shape[0]
    grid_spec = pltpu.PrefetchScalarGridSpec(
        num_scalar_prefetch=4,
        grid=(n_work,),
        in_specs=[pl.BlockSpec((tm, D_MODEL), lambda w, t, e, lo, hi: (t[w], 0)),
                  pl.BlockSpec((1, D_MODEL), lambda w, t, e, lo, hi: (0, 0)),
                  pl.BlockSpec((None, None, D_MODEL, D_EXPERT), lambda w, t, e, lo, hi: (l, e[w], 0, 0)),
                  pl.BlockSpec((None, None, D_MODEL, D_EXPERT), lambda w, t, e, lo, hi: (l, e[w], 0, 0)),
                  pl.BlockSpec((None, None, D_EXPERT, D_MODEL), lambda w, t, e, lo, hi: (l, e[w], 0, 0))],
        out_specs=pl.BlockSpec((tm, D_MODEL), lambda w, t, e, lo, hi: (t[w], 0)),
    )
    return pl.pallas_call(
        functools.partial(_experts_kernel, tm=tm),
        grid_spec=grid_spec,
        out_shape=jax.ShapeDtypeStruct(xs.shape, F32),
        compiler_params=_cparams(("arbitrary",), 48),
    )(tile, exp, lo, hi, xs, nw.reshape(1, D_MODEL), wg_bf, wu_bf, wd_bf)


def _combine_kernel(dest_ref, route_ref, x_ref, ys_ref, o_ref, buf0, buf1, sems, *, tm, n_tok):
    i = pl.program_id(0)

    def copies(step, slot, j):
        for u in range(ROW_UNROLL):
            r = j * ROW_UNROLL + u
            tok = step * tm + r
            yield _row_copy(ys_ref, dest_ref[tok], buf0.at[slot], r, sems.at[slot])
            yield _row_copy(ys_ref, dest_ref[n_tok + tok], buf1.at[slot], r, sems.at[slot])

    def issue(step, slot):
        def body(j, carry):
            for cp in copies(step, slot, j):
                cp.start()
            return carry

        lax.fori_loop(0, tm // ROW_UNROLL, body, 0)

    def drain(step, slot):
        def body(j, carry):
            for cp in copies(step, slot, j):
                cp.wait()
            return carry

        lax.fori_loop(0, tm // ROW_UNROLL, body, 0)

    slot = i % 2

    @pl.when(i == 0)
    def _():
        issue(0, 0)

    @pl.when(i + 1 < pl.num_programs(0))
    def _():
        issue(i + 1, 1 - slot)

    drain(i, slot)
    route = route_ref[...]
    o_ref[...] = x_ref[...] + route[:, 2:3] * buf0[slot] + route[:, 3:4] * buf1[slot]


def _combine(x2, route, ys, dest, tm):
    n = x2.shape[0]
    grid_spec = pltpu.PrefetchScalarGridSpec(
        num_scalar_prefetch=1,
        grid=(n // tm,),
        in_specs=[pl.BlockSpec((tm, LANES), lambda i, d: (i, 0)),
                  pl.BlockSpec((tm, D_MODEL), lambda i, d: (i, 0)),
                  pl.BlockSpec(memory_space=pl.ANY)],
        out_specs=pl.BlockSpec((tm, D_MODEL), lambda i, d: (i, 0)),
        scratch_shapes=[pltpu.VMEM((2, tm, D_MODEL), F32), pltpu.VMEM((2, tm, D_MODEL), F32),
                        pltpu.SemaphoreType.DMA((2,))],
    )
    return pl.pallas_call(
        functools.partial(_combine_kernel, tm=tm, n_tok=n),
        grid_spec=grid_spec,
        out_shape=jax.ShapeDtypeStruct((n, D_MODEL), F32),
        compiler_params=_cparams(("arbitrary",), 32),
    )(dest.reshape(2 * n), route, x2, ys)


def _ffn(x2, lw, l):
    n = x2.shape[0]
    route, route_t, cnt = _router(x2, lw["norm_ffn_w"], lw["w_rg"], lw["b_rg"], lw["w_re"], lw["b_re"])
    fields = route_t.astype(jnp.int32)
    tm_e = _pick(2 * n, 512) if n >= 4096 else 128
    tm_r = _pick(n, 256)
    dest, tile, exp, lo, hi = _moe_plan(fields[0], fields[1], fields[4], fields[5],
                                        cnt[0, :N_EXPERTS].astype(jnp.int32), tm_e)
    dest3 = dest.reshape(2, n // tm_r, tm_r).transpose(1, 0, 2)
    xs = _dispatch(x2, dest3, tm_r)
    ys = _experts(xs, lw["norm_ffn_w"], lw["wg"], lw["wu"], lw["wd"], l, tile, exp, lo, hi, tm_e)
    return _combine(x2, route, ys, dest, tm_r)


def _layer(x3, pos, l, depth, lw, ret_state, kv_cache, prev_kv):
    bsz, seq, _ = x3.shape
    n = bsz * seq
    sample = ret_state is not None
    lam_init = 0.8 - 0.6 * math.exp(-0.3 * l)
    x2 = x3.reshape(n, D_MODEL)
    proj2 = _inproj(x2, lw["norm_mix_w"], lw["w_in"], l)
    proj3 = proj2.reshape(bsz, seq, D_IN)

    a_y, a_v = _gmlp(proj2, seq, lw["a_norm_w"], lw["a_ws"], lw["a_bs_b"], emit_v=sample)
    b_y, s_new = _retention(proj3, pos, lw["b_norm_w"], ret_state)
    prep = _cprep(proj3, lw["c_qnorm_w"], lw["c_knorm_w"], lw["c_lambda"], lam_init, not sample, l, depth, prev_kv)
    qn, kn, kf_all, vf_all, lam_b = prep[:5]
    if sample:
        c_y = _dattn_sample(qn, kn, proj3, kv_cache[0], kv_cache[1], l, lam_b, lw["c_subln_w"], 1.0 - lam_init)
    else:
        c_y = _dattn_prompt(qn, kn, prep[5], lam_b, lw["c_subln_w"], 1.0 - lam_init)

    u = _merge(a_y, b_y.reshape(n, A_WIDTH), c_y.reshape(n, A_WIDTH), proj2, lw["w_branch"], l)
    x2 = _outproj(u, lw["w_out"], x2, l)
    x2 = _ffn(x2, lw, l)
    return x2.reshape(bsz, seq, D_MODEL), (kf_all, vf_all), s_new, a_v


def kernel(x_prompt, x_sample, cache_k_c, cache_v_c, state_ret, norm_mix_w, w_in, a_norm_w, a_ws, a_bs, b_norm_w, c_qnorm_w, c_knorm_w, c_lambda, c_subln_w, w_branch, w_out, norm_ffn_w, w_router_group, b_router_group, w_router_expert, b_router_expert, w_gate_e, w_up_e, w_down_e):
    depth = w_in.shape[0]
    past = cache_k_c.shape[2]
    pos_p = jnp.arange(x_prompt.shape[1])
    pos_s = past + jnp.arange(x_sample.shape[1])
    yp, ys = x_prompt, x_sample
    big = dict(w_in=w_in.astype(BF16), w_branch=w_branch.astype(BF16), w_out=w_out.astype(BF16),
               wg=w_gate_e.astype(BF16), wu=w_up_e.astype(BF16), wd=w_down_e.astype(BF16))
    kv_p = kv_s = None
    rets_p, rets_s, avs = [], [], []
    for l in range(depth):
        lw = dict(
            big, norm_mix_w=norm_mix_w[l], a_norm_w=a_norm_w[l], a_ws=a_ws[l],
            a_bs_b=jnp.broadcast_to(a_bs[l][:, :, None], (A_GROUPS, A_BLOCK, A_GROUP)),
            b_norm_w=b_norm_w[l], c_qnorm_w=c_qnorm_w[l], c_knorm_w=c_knorm_w[l], c_lambda=c_lambda[l],
            c_subln_w=c_subln_w[l], norm_ffn_w=norm_ffn_w[l], w_rg=w_router_group[l], b_rg=b_router_group[l],
            w_re=w_router_expert[l], b_re=b_router_expert[l])
        yp, kv_p, rp, _ = _layer(yp, pos_p, l, depth, lw, None, None, kv_p)
        ys, kv_s, rn, avn = _layer(ys, pos_s, l, depth, lw, state_ret[l], (cache_k_c, cache_v_c), kv_s)
        rets_p.append(rp)
        rets_s.append(rn)
        avs.append(avn.reshape(ys.shape[0], ys.shape[1], A_WIDTH))
    as_cache = lambda buf, x: buf.reshape(depth, x.shape[0], x.shape[1], C_HEADS, LANES)
    return (yp, ys, as_cache(kv_p[0], x_prompt), as_cache(kv_p[1], x_prompt), jnp.stack(rets_p, 0),
            as_cache(kv_s[0], x_sample), as_cache(kv_s[1], x_sample), jnp.stack(rets_s, 0), jnp.stack(avs, 0))
```

```python
import functools
import math

import jax
import jax.numpy as jnp
from jax import lax
from jax.experimental import pallas as pl
from jax.experimental.pallas import tpu as pltpu

F32 = jnp.float32
BF16 = jnp.bfloat16

D_MODEL = 2048
CHUNK = 64
A_WIDTH = 1024
A_BLOCK = 128
A_GROUP = 128
A_GROUPS = 8
B_HEADS = 4
B_DK = 128
B_DV = 256
ROPE_BASE = 10000.0
C_HEADS = 8
C_DH = 64
C_DV = 128
N_GROUPS = 4
EXPERTS_PER_GROUP = 4
N_EXPERTS = 16
D_EXPERT = 512
EPS = 1e-6
D_IN = 14336

OFF_AU, OFF_AV, OFF_BQ, OFF_BK, OFF_BV, OFF_BG, OFF_CQ, OFF_CK, OFF_CV, OFF_GATE = (
    0, 1024, 2048, 2560, 3072, 4096, 5120, 6144, 7168, 8192)

LANES = 128
MIB = 1024 * 1024
NEG_BIG = -1e30
Q_SCALE = (C_DH ** -0.5) * math.log2(math.e)
VT_ROWS = C_DV + 16
HEADS_PER_STEP = 4


def _cparams(sem, vmem_mib):
    return pltpu.CompilerParams(dimension_semantics=sem, vmem_limit_bytes=vmem_mib * MIB)


def _rms(x, w):
    ms = jnp.mean(x * x, axis=-1, keepdims=True)
    return x * lax.rsqrt(ms + EPS) * w


def _pick(n, pref):
    t = min(pref, n)
    while n % t:
        t //= 2
    return t


def _inproj_kernel(x_ref, g_ref, w_ref, o_ref, xn_ref):
    @pl.when(pl.program_id(1) == 0)
    def _():
        xn_ref[...] = _rms(x_ref[...], g_ref[...]).astype(BF16)

    o_ref[...] = jnp.dot(xn_ref[...], w_ref[...], preferred_element_type=F32).astype(o_ref.dtype)


def _inproj(x2, g, w_bf, l):
    n = x2.shape[0]
    tm = _pick(n, 1024)
    tn = 1024
    return pl.pallas_call(
        _inproj_kernel,
        grid=(n // tm, D_IN // tn),
        in_specs=[pl.BlockSpec((tm, D_MODEL), lambda i, j: (i, 0)),
                  pl.BlockSpec((1, D_MODEL), lambda i, j: (0, 0)),
                  pl.BlockSpec((None, D_MODEL, tn), lambda i, j: (l, 0, j))],
        out_specs=pl.BlockSpec((tm, tn), lambda i, j: (i, j)),
        out_shape=jax.ShapeDtypeStruct((n, D_IN), BF16),
        scratch_shapes=[pltpu.VMEM((tm, D_MODEL), BF16)],
        compiler_params=_cparams(("parallel", "arbitrary"), 48),
    )(x2, g.reshape(1, D_MODEL), w_bf)


def _gmlp_kernel(u_ref, v_ref, nw_ref, ws_ref, bs_ref, y_ref, *rest, pb, nblk, emit_v):
    u = jax.nn.gelu(u_ref[...].astype(F32))
    v = _rms(jax.nn.gelu(v_ref[...].astype(F32)), nw_ref[...])
    if emit_v:
        rest[0][...] = v
    vb = v.astype(BF16)
    p = lax.broadcasted_iota(jnp.int32, (A_BLOCK, A_BLOCK), 0)
    q = lax.broadcasted_iota(jnp.int32, (A_BLOCK, A_BLOCK), 1)
    mask = (q // CHUNK) <= (p // CHUNK)
    for g in range(A_GROUPS):
        w = jnp.where(mask, ws_ref[g], 0.0).astype(BF16)
        bias = bs_ref[g, :pb, :]
        cs = slice(g * A_GROUP, (g + 1) * A_GROUP)
        for b in range(nblk):
            rs = slice(b * pb, (b + 1) * pb)
            vg = vb[rs, cs]
            if pb < A_BLOCK:
                vg = jnp.concatenate([vg, jnp.zeros((A_BLOCK - pb, A_GROUP), BF16)], axis=0)
            s = jnp.dot(w, vg, preferred_element_type=F32)[:pb] + bias
            y_ref[rs, cs] = (u[rs, cs] * s).astype(y_ref.dtype)


def _gmlp(proj2, seq_len, nw, ws, bs_b, emit_v):
    n = proj2.shape[0]
    if seq_len % A_BLOCK == 0:
        pb, nblk = A_BLOCK, 4 if seq_len % (4 * A_BLOCK) == 0 else 1
    else:
        assert seq_len <= A_BLOCK
        pb, nblk = seq_len, 1
    r = pb * nblk
    out_shape = [jax.ShapeDtypeStruct((n, A_WIDTH), BF16)]
    out_specs = [pl.BlockSpec((r, A_WIDTH), lambda i: (i, 0))]
    if emit_v:
        out_shape.append(jax.ShapeDtypeStruct((n, A_WIDTH), F32))
        out_specs.append(pl.BlockSpec((r, A_WIDTH), lambda i: (i, 0)))
    res = pl.pallas_call(
        functools.partial(_gmlp_kernel, pb=pb, nblk=nblk, emit_v=emit_v),
        grid=(n // r,),
        in_specs=[pl.BlockSpec((r, A_WIDTH), lambda i: (i, OFF_AU // A_WIDTH)),
                  pl.BlockSpec((r, A_WIDTH), lambda i: (i, OFF_AV // A_WIDTH)),
                  pl.BlockSpec((1, A_WIDTH), lambda i: (0, 0)),
                  pl.BlockSpec((A_GROUPS, A_BLOCK, A_BLOCK), lambda i: (0, 0, 0)),
                  pl.BlockSpec((A_GROUPS, A_BLOCK, A_GROUP), lambda i: (0, 0, 0))],
        out_specs=out_specs,
        out_shape=out_shape,
        compiler_params=_cparams(("parallel",), 32),
    )(proj2, proj2, nw.reshape(1, A_WIDTH), ws, bs_b)
    return res if emit_v else (res[0], None)


def _ret_kernel(q_ref, k_ref, v_ref, g_ref, cos_ref, sin_ref, dm_ref, qd_ref, kd_ref, sd_ref, nw_ref, *rest,
                has_init):
    if has_init:
        s0_ref, y_ref, sfin_ref, s_scr = rest
    else:
        y_ref, sfin_ref, s_scr = rest
    c = pl.program_id(1)

    @pl.when(c == 0)
    def _():
        s_scr[...] = s0_ref[...] if has_init else jnp.zeros_like(s_scr)

    cos = cos_ref[...]
    sin = sin_ref[...]
    nw = nw_ref[...]
    for h in range(B_HEADS):
        ks = slice(h * B_DK, (h + 1) * B_DK)
        vs = slice(h * B_DV, (h + 1) * B_DV)
        q = q_ref[:, ks].astype(F32)
        k = k_ref[:, ks].astype(F32)
        qr = q * cos + pltpu.roll(q, B_DK // 2, 1) * sin
        kr = (k * cos + pltpu.roll(k, B_DK // 2, 1) * sin) * (B_DK ** -0.5)
        qb = qr.astype(BF16)
        kb = kr.astype(BF16)
        v = v_ref[:, vs]
        attn = lax.dot_general(qb, kb, (((1,), (1,)), ((), ())), preferred_element_type=F32) * dm_ref[h]
        intra = jnp.dot(attn.astype(BF16), v, preferred_element_type=F32)
        s_prev = s_scr[h]
        inter = jnp.dot(qb, s_prev.astype(BF16), preferred_element_type=F32) * qd_ref[h]
        o = intra + inter
        kt = jnp.transpose(kr * kd_ref[h]).astype(BF16)
        s_scr[h] = sd_ref[h] * s_prev + jnp.dot(kt, v, preferred_element_type=F32)
        gate = g_ref[:, vs].astype(F32)
        y_ref[:, vs] = (_rms(o, nw) * (gate * jax.nn.sigmoid(gate))).astype(y_ref.dtype)

    @pl.when(c == pl.num_programs(1) - 1)
    def _():
        sfin_ref[...] = s_scr[...]


def _ret_tables(c):
    lg = jnp.log1p(-jnp.exp2(-5.0 - jnp.arange(B_HEADS, dtype=F32)))
    idx = jnp.arange(c, dtype=F32)
    diff = idx[:, None] - idx[None, :]
    dm = jnp.where(diff >= 0, jnp.exp(jnp.maximum(diff, 0.0)[None] * lg[:, None, None]), 0.0)
    qd = jnp.exp((idx + 1.0)[None, :] * lg[:, None])
    kd = jnp.exp((c - 1.0 - idx)[None, :] * lg[:, None])
    sd = jnp.exp(c * lg)
    return (dm, jnp.broadcast_to(qd[:, :, None], (B_HEADS, c, B_DV)),
            jnp.broadcast_to(kd[:, :, None], (B_HEADS, c, B_DK)),
            jnp.broadcast_to(sd[:, None, None], (B_HEADS, 1, B_DV)))


def _rope_tables(pos):
    half = B_DK // 2
    inv_freq = 1.0 / (ROPE_BASE ** jnp.linspace(0.0, 1.0, half, dtype=F32))
    ang = pos.astype(F32)[:, None] * inv_freq[None, :]
    cos, sin = jnp.cos(ang), jnp.sin(ang)
    return jnp.concatenate([cos, cos], axis=-1), jnp.concatenate([-sin, sin], axis=-1)


def _retention(proj3, pos, nw, s0):
    bsz, seq, _ = proj3.shape
    c = _pick(seq, 256)
    dm, qd, kd, sd = _ret_tables(c)
    cos, sin = _rope_tables(pos)
    has_init = s0 is not None
    qk_w = B_HEADS * B_DK
    v_w = B_HEADS * B_DV
    full = lambda shape: pl.BlockSpec(shape, lambda b, i: (0,) * len(shape))
    state = pl.BlockSpec((None, B_HEADS, B_DK, B_DV), lambda b, i: (b, 0, 0, 0))
    in_specs = [pl.BlockSpec((None, c, qk_w), lambda b, i: (b, i, OFF_BQ // qk_w)),
                pl.BlockSpec((None, c, qk_w), lambda b, i: (b, i, OFF_BK // qk_w)),
                pl.BlockSpec((None, c, v_w), lambda b, i: (b, i, OFF_BV // v_w)),
                pl.BlockSpec((None, c, v_w), lambda b, i: (b, i, OFF_BG // v_w)),
                pl.BlockSpec((c, B_DK), lambda b, i: (i, 0)),
                pl.BlockSpec((c, B_DK), lambda b, i: (i, 0)),
                full((B_HEADS, c, c)), full((B_HEADS, c, B_DV)), full((B_HEADS, c, B_DK)),
                full((B_HEADS, 1, B_DV)), full((1, B_DV))]
    args = [proj3, proj3, proj3, proj3, cos, sin, dm, qd, kd, sd, nw.reshape(1, B_DV)]
    if has_init:
        in_specs.append(state)
        args.append(s0)
    return pl.pallas_call(
        functools.partial(_ret_kernel, has_init=has_init),
        grid=(bsz, seq // c),
        in_specs=in_specs,
        out_specs=[pl.BlockSpec((None, c, v_w), lambda b, i: (b, i, 0)), state],
        out_shape=[jax.ShapeDtypeStruct((bsz, seq, v_w), BF16),
                   jax.ShapeDtypeStruct((bsz, B_HEADS, B_DK, B_DV), F32)],
        scratch_shapes=[pltpu.VMEM((B_HEADS, B_DK, B_DV), F32)],
        compiler_params=_cparams(("parallel", "arbitrary"), 32),
    )(*args)


def _group_mean_matrix():
    r = lax.broadcasted_iota(jnp.int32, (LANES, LANES), 0)
    c = lax.broadcasted_iota(jnp.int32, (LANES, LANES), 1)
    return jnp.where((r // C_DH) == (c // C_DH), 1.0 / C_DH, 0.0).astype(BF16)


def _cprep_kernel(q_ref, k_ref, v_ref, qw_ref, kw_ref, lam_ref, *rest, lam_init, emit_vt, n_prev, l):
    qn_ref, kn_ref, kf_ref, vf_ref, lamo_ref = rest[n_prev:n_prev + 5]
    rows = q_ref.shape[0]
    if n_prev == 0:
        for d in range(kf_ref.shape[0]):
            if d != l:
                kf_ref[d] = jnp.zeros(kf_ref.shape[1:], F32)
                vf_ref[d] = jnp.zeros(vf_ref.shape[1:], F32)
        kf_ref, vf_ref = kf_ref.at[l], vf_ref.at[l]
    gm = _group_mean_matrix()
    qw = qw_ref[...]
    kw = kw_ref[...]
    for h in range(C_HEADS):
        cs = slice(h * LANES, (h + 1) * LANES)
        hs = pl.ds(h, rows, stride=C_HEADS)
        x = q_ref[:, cs].astype(F32)
        ms = jnp.dot((x * x).astype(BF16), gm, preferred_element_type=F32)
        qn_ref[:, cs] = (x * lax.rsqrt(ms + EPS) * qw * Q_SCALE).astype(BF16)
        x = k_ref[:, cs].astype(F32)
        ms = jnp.dot((x * x).astype(BF16), gm, preferred_element_type=F32)
        kn = x * lax.rsqrt(ms + EPS) * kw
        kf_ref[hs, :] = kn
        kn_ref[:, cs] = kn.astype(BF16)
        vv = v_ref[:, cs].astype(F32)
        vf_ref[hs, :] = vv
        if emit_vt:
            vt_ref = rest[n_prev + 5]
            vt_ref[h, :C_DV, :] = jnp.transpose(vv).astype(BF16)
            vt_ref[h, C_DV:, :] = jnp.ones((VT_ROWS - C_DV, rows), BF16)
    lq = lam_ref[...]
    l01 = jnp.sum(lq[0:1] * lq[1:2], axis=-1, keepdims=True)
    l23 = jnp.sum(lq[2:3] * lq[3:4], axis=-1, keepdims=True)
    lam = jnp.exp(l01) - jnp.exp(l23) + lam_init
    lamo_ref[...] = jnp.broadcast_to(lam, lamo_ref.shape)


def _cprep(proj3, qw, kw, lam_p, lam_init, emit_vt, l, depth, prev_kv):
    bsz, seq, _ = proj3.shape
    r = _pick(seq, 512)
    w = C_HEADS * LANES
    qw2 = jnp.concatenate([qw, qw]).reshape(1, LANES)
    kw2 = jnp.concatenate([kw, kw]).reshape(1, LANES)
    blk = lambda off: pl.BlockSpec((None, r, w), lambda b, i: (b, i, off // w))
    row = pl.BlockSpec((None, r, w), lambda b, i: (b, i, 0))
    if prev_kv is None:
        kv = pl.BlockSpec((depth, None, r * C_HEADS, LANES), lambda b, i: (0, b, i, 0))
    else:
        kv = pl.BlockSpec((None, None, r * C_HEADS, LANES), lambda b, i: (l, b, i, 0))
    kv_shape = jax.ShapeDtypeStruct((depth, bsz, seq * C_HEADS, LANES), F32)
    out_shape = [jax.ShapeDtypeStruct((bsz, seq, w), BF16), jax.ShapeDtypeStruct((bsz, seq, w), BF16),
                 kv_shape, kv_shape, jax.ShapeDtypeStruct((8, LANES), F32)]
    out_specs = [row, row, kv, kv, pl.BlockSpec((8, LANES), lambda b, i: (0, 0))]
    if emit_vt:
        out_shape.append(jax.ShapeDtypeStruct((bsz, C_HEADS, VT_ROWS, seq), BF16))
        out_specs.append(pl.BlockSpec((None, C_HEADS, VT_ROWS, r), lambda b, i: (b, 0, 0, i)))
    in_specs = [blk(OFF_CQ), blk(OFF_CK), blk(OFF_CV),
                pl.BlockSpec((1, LANES), lambda b, i: (0, 0)),
                pl.BlockSpec((1, LANES), lambda b, i: (0, 0)),
                pl.BlockSpec((4, C_DH), lambda b, i: (0, 0))]
    args = [proj3, proj3, proj3, qw2, kw2, lam_p]
    aliases = {}
    if prev_kv is not None:
        in_specs += [pl.BlockSpec(memory_space=pl.ANY)] * 2
        aliases = {len(args): 2, len(args) + 1: 3}
        args += list(prev_kv)
    return pl.pallas_call(
        functools.partial(_cprep_kernel, lam_init=lam_init, emit_vt=emit_vt, n_prev=len(aliases), l=l),
        grid=(bsz, seq // r),
        in_specs=in_specs,
        out_specs=out_specs,
        out_shape=out_shape,
        input_output_aliases=aliases,
        compiler_params=_cparams(("arbitrary", "arbitrary"), 40),
    )(*args)


def _dattn_kernel(q_ref, k_ref, vt_ref, lam_ref, sw_ref, o_ref, s_a, s_b, m_scr, a_scr, *, t, out_scale):
    qi = pl.program_id(2)
    lane = lax.broadcasted_iota(jnp.int32, (t, LANES), 1)
    qc = []
    for hh in range(HEADS_PER_STEP):
        q = q_ref[:, hh * LANES:(hh + 1) * LANES]
        qc += [jnp.where(lane < C_DH, q, jnp.zeros_like(q)), jnp.where(lane >= C_DH, q, jnp.zeros_like(q))]
    m_scr[...] = jnp.full(m_scr.shape, NEG_BIG, F32)
    a_scr[...] = jnp.zeros(a_scr.shape, F32)

    def scores(blk, s_ref):
        rows = pl.ds(pl.multiple_of(blk * t, t), t)
        for ch in range(2 * HEADS_PER_STEP):
            hh = ch // 2
            kb = k_ref[rows, hh * LANES:(hh + 1) * LANES]
            s_ref[ch] = lax.dot_general(kb, qc[ch], (((1,), (1,)), ((), ())), preferred_element_type=F32)

    def update(blk, s_ref, masked=False):
        cols = pl.ds(pl.multiple_of(blk * t, t), t)
        if masked:
            kpos = lax.broadcasted_iota(jnp.int32, (t, t), 0)
            qpos = lax.broadcasted_iota(jnp.int32, (t, t), 1)
            vis = (kpos // CHUNK) <= (qpos // CHUNK)
        for ch in range(2 * HEADS_PER_STEP):
            s = s_ref[ch]
            if masked:
                s = jnp.where(vis, s, NEG_BIG)
            m_old = m_scr[ch]
            m_new = jnp.maximum(m_old, jnp.max(s, axis=0, keepdims=True))
            alpha = jnp.exp2(m_old - m_new)
            p = jnp.exp2(s - m_new).astype(BF16)
            a_scr[ch] = alpha * a_scr[ch] + jnp.dot(vt_ref[ch // 2, :, cols], p, preferred_element_type=F32)
            m_scr[ch] = m_new

    scores(qi, s_a)

    @pl.when(qi == 0)
    def _():
        update(qi, s_a, masked=True)

    @pl.when(qi > 0)
    def _():
        scores(0, s_b)
        update(qi, s_a, masked=True)

        def pair(jj, carry):
            j = 2 * jj
            scores(j + 1, s_a)
            update(j, s_b)
            scores(j + 2, s_b)
            update(j + 1, s_a)
            return carry

        lax.fori_loop(0, (qi - 1) // 2, pair, 0)

        @pl.when(qi % 2 == 1)
        def _():
            update(qi - 1, s_b)

        @pl.when(qi % 2 == 0)
        def _():
            scores(qi - 1, s_a)
            update(qi - 2, s_b)
            update(qi - 1, s_a)

    lam = lam_ref[0:1, 0:1]
    for hh in range(HEADS_PER_STEP):
        c0, c1 = 2 * hh, 2 * hh + 1
        o_t = (a_scr[c0, :C_DV, :] / a_scr[c0, C_DV:C_DV + 1, :]
               - lam * (a_scr[c1, :C_DV, :] / a_scr[c1, C_DV:C_DV + 1, :]))
        ms = jnp.mean(o_t * o_t, axis=0, keepdims=True)
        o = jnp.transpose(o_t * lax.rsqrt(ms + EPS))
        o_ref[:, hh * C_DV:(hh + 1) * C_DV] = (o * sw_ref[...] * out_scale).astype(o_ref.dtype)


def _dattn_prompt(qn, kn, vt, lam_b, sw, out_scale):
    bsz, seq, w = qn.shape
    t = _pick(seq, 512)
    hw = HEADS_PER_STEP * LANES
    chains = 2 * HEADS_PER_STEP
    return pl.pallas_call(
        functools.partial(_dattn_kernel, t=t, out_scale=out_scale),
        grid=(bsz, C_HEADS // HEADS_PER_STEP, seq // t),
        in_specs=[pl.BlockSpec((None, t, hw), lambda b, h, i: (b, i, h)),
                  pl.BlockSpec((None, seq, hw), lambda b, h, i: (b, 0, h), pipeline_mode=pl.Buffered(1)),
                  pl.BlockSpec((None, HEADS_PER_STEP, VT_ROWS, seq), lambda b, h, i: (b, h, 0, 0),
                               pipeline_mode=pl.Buffered(1)),
                  pl.BlockSpec((8, LANES), lambda b, h, i: (0, 0)),
                  pl.BlockSpec((1, C_DV), lambda b, h, i: (0, 0))],
        out_specs=pl.BlockSpec((None, t, hw), lambda b, h, i: (b, i, h)),
        out_shape=jax.ShapeDtypeStruct((bsz, seq, w), BF16),
        scratch_shapes=[pltpu.VMEM((chains, t, t), F32), pltpu.VMEM((chains, t, t), F32),
                        pltpu.VMEM((chains, 1, t), F32), pltpu.VMEM((chains, VT_ROWS, t), F32)],
        compiler_params=_cparams(("parallel", "parallel", "arbitrary"), 52),
    )(qn, kn, vt, lam_b, sw.reshape(1, C_DV))


def _dattn_sample_kernel(q_ref, kc_ref, vc_ref, kn_ref, vn_ref, lam_ref, sw_ref, o_ref, *, past, out_scale):
    lq = q_ref.shape[0]
    lane = lax.broadcasted_iota(jnp.int32, (lq, LANES), 1)
    qpos_c = past + lax.broadcasted_iota(jnp.int32, (lq, past), 0)
    kpos_c = lax.broadcasted_iota(jnp.int32, (lq, past), 1)
    vis_c = (kpos_c // CHUNK) <= (qpos_c // CHUNK)
    qpos_n = past + lax.broadcasted_iota(jnp.int32, (lq, lq), 0)
    kpos_n = past + lax.broadcasted_iota(jnp.int32, (lq, lq), 1)
    vis_n = (kpos_n // CHUNK) <= (qpos_n // CHUNK)
    lam = lam_ref[0:1, 0:1]
    nt = (((1,), (1,)), ((), ()))
    for h in range(C_HEADS):
        cs = slice(h * LANES, (h + 1) * LANES)
        q = q_ref[:, cs]
        kc = kc_ref[pl.ds(h, past, stride=C_HEADS), :].astype(BF16)
        vc = vc_ref[pl.ds(h, past, stride=C_HEADS), :].astype(BF16)
        kn = kn_ref[:, cs]
        vn = vn_ref[:, cs]
        outs = []
        for c in range(2):
            qc = jnp.where((lane // C_DH) == c, q, jnp.zeros_like(q))
            s_c = jnp.where(vis_c, lax.dot_general(qc, kc, nt, preferred_element_type=F32), NEG_BIG)
            s_n = jnp.where(vis_n, lax.dot_general(qc, kn, nt, preferred_element_type=F32), NEG_BIG)
            m = jnp.maximum(jnp.max(s_c, axis=-1, keepdims=True), jnp.max(s_n, axis=-1, keepdims=True))
            p_c = jnp.exp2(s_c - m)
            p_n = jnp.exp2(s_n - m)
            l = jnp.sum(p_c, axis=-1, keepdims=True) + jnp.sum(p_n, axis=-1, keepdims=True)
            acc = (jnp.dot(p_c.astype(BF16), vc, preferred_element_type=F32)
                   + jnp.dot(p_n.astype(BF16), vn, preferred_element_type=F32))
            outs.append(acc / l)
        o = outs[0] - lam * outs[1]
        o_ref[:, cs] = (_rms(o, sw_ref[...]) * out_scale).astype(o_ref.dtype)


def _dattn_sample(qn, kn, proj3, cache_k, cache_v, l, lam_b, sw, out_scale):
    bsz, lq, w = qn.shape
    depth, _, past = cache_k.shape[:3]
    ck = cache_k.reshape(depth, bsz, past * C_HEADS, LANES)
    cv = cache_v.reshape(depth, bsz, past * C_HEADS, LANES)
    rows = lambda r: pl.BlockSpec((None, r, w), lambda b: (b, 0, 0))
    cache = pl.BlockSpec((None, None, past * C_HEADS, LANES), lambda b: (l, b, 0, 0))
    return pl.pallas_call(
        functools.partial(_dattn_sample_kernel, past=past, out_scale=out_scale),
        grid=(bsz,),
        in_specs=[rows(lq), cache, cache, rows(lq),
                  pl.BlockSpec((None, lq, w), lambda b: (b, 0, OFF_CV // w)),
                  pl.BlockSpec((8, LANES), lambda b: (0, 0)),
                  pl.BlockSpec((1, C_DV), lambda b: (0, 0))],
        out_specs=rows(lq),
        out_shape=jax.ShapeDtypeStruct((bsz, lq, w), BF16),
        compiler_params=_cparams(("parallel",), 40),
    )(qn, ck, cv, kn, proj3, lam_b, sw.reshape(1, C_DV))


def _merge_kernel(ya_ref, yb_ref, yc_ref, ga_ref, gb_ref, gc_ref, w_ref, o_ref):
    acc = None
    for y_ref, g_ref, n in ((ya_ref, ga_ref, 0), (yb_ref, gb_ref, 1), (yc_ref, gc_ref, 2)):
        up = jnp.dot(y_ref[...], w_ref[n], preferred_element_type=F32)
        term = jax.nn.sigmoid(g_ref[...].astype(F32)) * up
        acc = term if acc is None else acc + term
    o_ref[...] = acc.astype(o_ref.dtype)


def _merge(ya, yb, yc, proj2, wb_bf, l):
    n = ya.shape[0]
    tm = _pick(n, 512)
    tn = 1024
    ybs = pl.BlockSpec((tm, A_WIDTH), lambda j, i: (i, 0))
    gate = lambda b: pl.BlockSpec((tm, tn), lambda j, i: (i, (OFF_GATE + b * D_MODEL) // tn + j))
    return pl.pallas_call(
        _merge_kernel,
        grid=(D_MODEL // tn, n // tm),
        in_specs=[ybs, ybs, ybs, gate(0), gate(1), gate(2),
                  pl.BlockSpec((None, 3, A_WIDTH, tn), lambda j, i: (l, 0, 0, j))],
        out_specs=pl.BlockSpec((tm, tn), lambda j, i: (i, j)),
        out_shape=jax.ShapeDtypeStruct((n, D_MODEL), BF16),
        compiler_params=_cparams(("parallel", "parallel"), 48),
    )(ya, yb, yc, proj2, proj2, proj2, wb_bf)


def _outproj_kernel(u_ref, w_ref, x_ref, o_ref):
    o_ref[...] = x_ref[...] + jnp.dot(u_ref[...], w_ref[...], preferred_element_type=F32)


def _outproj(u, w_bf, x2, l):
    n = u.shape[0]
    tm = _pick(n, 512)
    return pl.pallas_call(
        _outproj_kernel,
        grid=(n // tm,),
        in_specs=[pl.BlockSpec((tm, D_MODEL), lambda i: (i, 0)),
                  pl.BlockSpec((None, D_MODEL, D_MODEL), lambda i: (l, 0, 0)),
                  pl.BlockSpec((tm, D_MODEL), lambda i: (i, 0))],
        out_specs=pl.BlockSpec((tm, D_MODEL), lambda i: (i, 0)),
        out_shape=jax.ShapeDtypeStruct((n, D_MODEL), F32),
        compiler_params=_cparams(("parallel",), 48),
    )(u, w_bf, x2)


ROUTE_GROUP_LANE = N_EXPERTS


def _router_kernel(x_ref, nw_ref, wcat_ref, br_ref, o_ref, ot_ref, cnt_ref, run_scr):
    @pl.when(pl.program_id(0) == 0)
    def _():
        run_scr[...] = jnp.zeros_like(run_scr)

    tm = x_ref.shape[0]
    xn = _rms(x_ref[...], nw_ref[...])
    hi = xn.astype(BF16)
    lo = (xn - hi.astype(F32)).astype(BF16)
    both = jnp.dot(hi, wcat_ref[...], preferred_element_type=F32)
    logit = (both[:, :LANES] + both[:, LANES:]
             + jnp.dot(lo, wcat_ref[:, :LANES], preferred_element_type=F32) + br_ref[...])
    lane = lax.broadcasted_iota(jnp.int32, logit.shape, 1).astype(F32)
    big = 1e9
    is_g = (lane >= ROUTE_GROUP_LANE) & (lane < ROUTE_GROUP_LANE + N_GROUPS)
    lg = jnp.where(is_g, logit, NEG_BIG)
    mg = jnp.max(lg, axis=-1, keepdims=True)
    g_sel = jnp.min(jnp.where(lg == mg, lane - ROUTE_GROUP_LANE, big), axis=-1, keepdims=True)
    p_group = 1.0 / jnp.sum(jnp.exp(lg - mg), axis=-1, keepdims=True)
    lo_lane = g_sel * EXPERTS_PER_GROUP
    in_g = (lane >= lo_lane) & (lane < lo_lane + EXPERTS_PER_GROUP)
    le = jnp.where(in_g, logit, NEG_BIG)
    v1 = jnp.max(le, axis=-1, keepdims=True)
    i1 = jnp.min(jnp.where(le == v1, lane, big), axis=-1, keepdims=True)
    le2 = jnp.where(lane == i1, NEG_BIG, le)
    v2 = jnp.max(le2, axis=-1, keepdims=True)
    i2 = jnp.min(jnp.where(le2 == v2, lane, big), axis=-1, keepdims=True)
    e2 = jnp.exp(v2 - v1)
    den = 1.0 + e2
    p1 = p_group / den
    p2 = p_group * e2 / den
    oh1 = jnp.where(lane == i1, 1.0, 0.0)
    oh2 = jnp.where(lane == i2, 1.0, 0.0)
    oh = oh1 + oh2
    row = lax.broadcasted_iota(jnp.int32, (tm, tm), 0)
    col = lax.broadcasted_iota(jnp.int32, (tm, tm), 1)
    tri = jnp.where(row > col, 1.0, 0.0).astype(BF16)
    before = run_scr[...] + jnp.dot(tri, oh.astype(BF16), preferred_element_type=F32)
    r1 = jnp.sum(oh1 * before, axis=-1, keepdims=True)
    r2 = jnp.sum(oh2 * before, axis=-1, keepdims=True)
    run = run_scr[...] + jnp.sum(oh, axis=0, keepdims=True)
    run_scr[...] = run
    cnt_ref[...] = jnp.broadcast_to(run, cnt_ref.shape)
    out = jnp.zeros_like(logit)
    for n, val in enumerate((i1, i2, p1, p2, r1, r2)):
        out = jnp.where(lane == n, val, out)
    o_ref[...] = out
    ot_ref[...] = jnp.transpose(out)[:8]


def _router(x2, nw, w_rg, b_rg, w_re, b_re):
    n = x2.shape[0]
    tm = _pick(n, 512)
    w = jnp.zeros((D_MODEL, LANES), F32).at[:, :N_EXPERTS].set(w_re).at[:, N_EXPERTS:N_EXPERTS + N_GROUPS].set(w_rg)
    br = jnp.zeros((1, LANES), F32).at[0, :N_EXPERTS].set(b_re).at[0, N_EXPERTS:N_EXPERTS + N_GROUPS].set(b_rg)
    whi = w.astype(BF16)
    wlo = (w - whi.astype(F32)).astype(BF16)
    full = lambda shape: pl.BlockSpec(shape, lambda i: (0, 0))
    return pl.pallas_call(
        _router_kernel,
        grid=(n // tm,),
        in_specs=[pl.BlockSpec((tm, D_MODEL), lambda i: (i, 0)), full((1, D_MODEL)),
                  full((D_MODEL, 2 * LANES)), full((1, LANES))],
        out_specs=[pl.BlockSpec((tm, LANES), lambda i: (i, 0)), pl.BlockSpec((8, tm), lambda i: (0, i)),
                   full((8, LANES))],
        out_shape=[jax.ShapeDtypeStruct((n, LANES), F32), jax.ShapeDtypeStruct((8, n), F32),
                   jax.ShapeDtypeStruct((8, LANES), F32)],
        scratch_shapes=[pltpu.VMEM((1, LANES), F32)],
        compiler_params=_cparams(("arbitrary",), 32),
    )(x2, nw.reshape(1, D_MODEL), jnp.concatenate([whi, wlo], axis=1), br)


def _moe_plan(e1, e2, r1, r2, cnt, tm_e):
    n = e1.shape[0]
    ar = jnp.arange(N_EXPERTS, dtype=jnp.int32)
    offs = jnp.cumsum(cnt) - cnt
    dest = jnp.stack([jnp.take(offs, e1) + r1, jnp.take(offs, e2) + r2])
    n_tiles = (2 * n) // tm_e
    first = offs // tm_e
    last = jnp.where(cnt > 0, (offs + cnt - 1) // tm_e, first - 1)
    n_items = last - first + 1
    item_end = jnp.cumsum(n_items)
    item_start = item_end - n_items
    n_work = n_tiles + N_EXPERTS - 1
    w = jnp.arange(n_work, dtype=jnp.int32)
    e_w = jnp.minimum(jnp.sum((w[:, None] >= item_end[None, :]).astype(jnp.int32), axis=1), N_EXPERTS - 1)
    valid = w < item_end[-1]
    ohw = (e_w[:, None] == ar[None, :]).astype(jnp.int32)
    sel = lambda tab: jnp.sum(ohw * tab[None, :], axis=1)
    tile = sel(first) + (w - sel(item_start))
    lo = jnp.maximum(sel(offs), tile * tm_e)
    hi = jnp.minimum(sel(offs) + sel(cnt), (tile + 1) * tm_e)
    last_e = jnp.max(jnp.where(n_items > 0, ar, 0))
    tile = jnp.where(valid, tile, n_tiles - 1)
    e_w = jnp.where(valid, e_w, last_e)
    lo = jnp.where(valid, lo, 0)
    hi = jnp.where(valid, hi, 0)
    return dest.astype(jnp.int32), tile.astype(jnp.int32), e_w.astype(jnp.int32), lo.astype(jnp.int32), hi.astype(jnp.int32)


def _row_copy(src_ref, src_row, dst_ref, dst_row, sem):
    return pltpu.make_async_copy(src_ref.at[pl.ds(src_row, 1), :], dst_ref.at[pl.ds(dst_row, 1), :], sem)


ROW_UNROLL = 8


def _dispatch_kernel(dest_ref, x_ref, xs_ref, sem, *, tm):
    def copies(i):
        for u in range(ROW_UNROLL):
            r = i * ROW_UNROLL + u
            for k in range(2):
                yield _row_copy(x_ref, r, xs_ref, dest_ref[k, r], sem)

    def issue(i, carry):
        for cp in copies(i):
            cp.start()
        return carry

    def drain(i, carry):
        for cp in copies(i):
            cp.wait()
        return carry

    lax.fori_loop(0, tm // ROW_UNROLL, issue, 0)
    lax.fori_loop(0, tm // ROW_UNROLL, drain, 0)


def _dispatch(x2, dest3, tm):
    n = x2.shape[0]
    return pl.pallas_call(
        functools.partial(_dispatch_kernel, tm=tm),
        grid=(n // tm,),
        in_specs=[pl.BlockSpec((None, 2, tm), lambda i: (i, 0, 0), memory_space=pltpu.SMEM),
                  pl.BlockSpec((tm, D_MODEL), lambda i: (i, 0))],
        out_specs=pl.BlockSpec(memory_space=pl.ANY),
        out_shape=jax.ShapeDtypeStruct((2 * n, D_MODEL), F32),
        scratch_shapes=[pltpu.SemaphoreType.DMA(())],
        compiler_params=_cparams(("arbitrary",), 32),
    )(dest3, x2)


def _experts_kernel(tile_ref, exp_ref, lo_ref, hi_ref, xs_ref, nw_ref, wg_ref, wu_ref, wd_ref, ys_ref, *, tm):
    w = pl.program_id(0)
    lo = lo_ref[w]
    hi = hi_ref[w]
    tile = tile_ref[w]
    first_visit = jnp.logical_or(w == 0, tile_ref[jnp.maximum(w - 1, 0)] != tile)

    @pl.when(hi > lo)
    def _():
        xn = _rms(xs_ref[...], nw_ref[...]).astype(BF16)
        gate = jnp.dot(xn, wg_ref[...].astype(BF16), preferred_element_type=F32)
        up = jnp.dot(xn, wu_ref[...].astype(BF16), preferred_element_type=F32)
        h = (gate * jax.nn.sigmoid(gate) * up).astype(BF16)
        y = jnp.dot(h, wd_ref[...].astype(BF16), preferred_element_type=F32)

        @pl.when(first_visit)
        def _():
            ys_ref[...] = y

        @pl.when(jnp.logical_not(first_visit))
        def _():
            rows = tile * tm + lax.broadcasted_iota(jnp.int32, (tm, 1), 0)
            ys_ref[...] = jnp.where((rows >= lo) & (rows < hi), y, ys_ref[...])


def _experts(xs, nw, wg_bf, wu_bf, wd_bf, l, tile, exp, lo, hi, tm):
    n_work = tile.shape[0]
    grid_spec = pltpu.PrefetchScalarGridSpec(
        num_scalar_prefetch=4,
        grid=(n_work,),
        in_specs=[pl.BlockSpec((tm, D_MODEL), lambda w, t, e, lo, hi: (t[w], 0)),
                  pl.BlockSpec((1, D_MODEL), lambda w, t, e, lo, hi: (0, 0)),
                  pl.BlockSpec((None, None, D_MODEL, D_EXPERT), lambda w, t, e, lo, hi: (l, e[w], 0, 0)),
                  pl.BlockSpec((None, None, D_MODEL, D_EXPERT), lambda w, t, e, lo, hi: (l, e[w], 0, 0)),
                  pl.BlockSpec((None, None, D_EXPERT, D_MODEL), lambda w, t, e, lo, hi: (l, e[w], 0, 0))],
        out_specs=pl.BlockSpec((tm, D_MODEL), lambda w, t, e, lo, hi: (t[w], 0)),
    )
    return pl.pallas_call(
        functools.partial(_experts_kernel, tm=tm),
        grid_spec=grid_spec,
        out_shape=jax.ShapeDtypeStruct(xs.shape, F32),
        compiler_params=_cparams(("arbitrary",), 56),
    )(tile, exp, lo, hi, xs, nw.reshape(1, D_MODEL), wg_bf, wu_bf, wd_bf)


def _combine_kernel(dest_ref, route_ref, x_ref, ys_ref, o_ref, buf0, buf1, sems, *, tm, n_tok):
    i = pl.program_id(0)

    def copies(step, slot, j):
        for u in range(ROW_UNROLL):
            r = j * ROW_UNROLL + u
            tok = step * tm + r
            yield _row_copy(ys_ref, dest_ref[tok], buf0.at[slot], r, sems.at[slot])
            yield _row_copy(ys_ref, dest_ref[n_tok + tok], buf1.at[slot], r, sems.at[slot])

    def issue(step, slot):
        def body(j, carry):
            for cp in copies(step, slot, j):
                cp.start()
            return carry

        lax.fori_loop(0, tm // ROW_UNROLL, body, 0)

    def drain(step, slot):
        def body(j, carry):
            for cp in copies(step, slot, j):
                cp.wait()
            return carry

        lax.fori_loop(0, tm // ROW_UNROLL, body, 0)

    slot = i % 2

    @pl.when(i == 0)
    def _():
        issue(0, 0)

    @pl.when(i + 1 < pl.num_programs(0))
    def _():
        issue(i + 1, 1 - slot)

    drain(i, slot)
    route = route_ref[...]
    o_ref[...] = x_ref[...] + route[:, 2:3] * buf0[slot] + route[:, 3:4] * buf1[slot]


def _combine(x2, route, ys, dest, tm):
    n = x2.shape[0]
    grid_spec = pltpu.PrefetchScalarGridSpec(
        num_scalar_prefetch=1,
        grid=(n // tm,),
        in_specs=[pl.BlockSpec((tm, LANES), lambda i, d: (i, 0)),
                  pl.BlockSpec((tm, D_MODEL), lambda i, d: (i, 0)),
                  pl.BlockSpec(memory_space=pl.ANY)],
        out_specs=pl.BlockSpec((tm, D_MODEL), lambda i, d: (i, 0)),
        scratch_shapes=[pltpu.VMEM((2, tm, D_MODEL), F32), pltpu.VMEM((2, tm, D_MODEL), F32),
                        pltpu.SemaphoreType.DMA((2,))],
    )
    return pl.pallas_call(
        functools.partial(_combine_kernel, tm=tm, n_tok=n),
        grid_spec=grid_spec,
        out_shape=jax.ShapeDtypeStruct((n, D_MODEL), F32),
        compiler_params=_cparams(("arbitrary",), 32),
    )(dest.reshape(2 * n), route, x2, ys)


def _ffn(x2, lw, l):
    n = x2.shape[0]
    route, route_t, cnt = _router(x2, lw["norm_ffn_w"], lw["w_rg"], lw["b_rg"], lw["w_re"], lw["b_re"])
    fields = route_t.astype(jnp.int32)
    tm_e = _pick(2 * n, 512) if n >= 4096 else 128
    tm_r = _pick(n, 256)
    dest, tile, exp, lo, hi = _moe_plan(fields[0], fields[1], fields[4], fields[5],
                                        cnt[0, :N_EXPERTS].astype(jnp.int32), tm_e)
    dest3 = dest.reshape(2, n // tm_r, tm_r).transpose(1, 0, 2)
    xs = _dispatch(x2, dest3, tm_r)
    ys = _experts(xs, lw["norm_ffn_w"], lw["wg"], lw["wu"], lw["wd"], l, tile, exp, lo, hi, tm_e)
    return _combine(x2, route, ys, dest, tm_r)


def _layer(x3, pos, l, depth, lw, ret_state, kv_cache, prev_kv):
    bsz, seq, _ = x3.shape
    n = bsz * seq
    sample = ret_state is not None
    lam_init = 0.8 - 0.6 * math.exp(-0.3 * l)
    x2 = x3.reshape(n, D_MODEL)
    proj2 = _inproj(x2, lw["norm_mix_w"], lw["w_in"], l)
    proj3 = proj2.reshape(bsz, seq, D_IN)

    a_y, a_v = _gmlp(proj2, seq, lw["a_norm_w"], lw["a_ws"], lw["a_bs_b"], emit_v=sample)
    b_y, s_new = _retention(proj3, pos, lw["b_norm_w"], ret_state)
    prep = _cprep(proj3, lw["c_qnorm_w"], lw["c_knorm_w"], lw["c_lambda"], lam_init, not sample, l, depth, prev_kv)
    qn, kn, kf_all, vf_all, lam_b = prep[:5]
    if sample:
        c_y = _dattn_sample(qn, kn, proj3, kv_cache[0], kv_cache[1], l, lam_b, lw["c_subln_w"], 1.0 - lam_init)
    else:
        c_y = _dattn_prompt(qn, kn, prep[5], lam_b, lw["c_subln_w"], 1.0 - lam_init)

    u = _merge(a_y, b_y.reshape(n, A_WIDTH), c_y.reshape(n, A_WIDTH), proj2, lw["w_branch"], l)
    x2 = _outproj(u, lw["w_out"], x2, l)
    x2 = _ffn(x2, lw, l)
    return x2.reshape(bsz, seq, D_MODEL), (kf_all, vf_all), s_new, a_v


def kernel(x_prompt, x_sample, cache_k_c, cache_v_c, state_ret, norm_mix_w, w_in, a_norm_w, a_ws, a_bs, b_norm_w, c_qnorm_w, c_knorm_w, c_lambda, c_subln_w, w_branch, w_out, norm_ffn_w, w_router_group, b_router_group, w_router_expert, b_router_expert, w_gate_e, w_up_e, w_down_e):
    depth = w_in.shape[0]
    past = cache_k_c.shape[2]
    pos_p = jnp.arange(x_prompt.shape[1])
    pos_s = past + jnp.arange(x_sample.shape[1])
    yp, ys = x_prompt, x_sample
    big = dict(w_in=w_in.astype(BF16), w_branch=w_branch.astype(BF16), w_out=w_out.astype(BF16),
               wg=w_gate_e, wu=w_up_e, wd=w_down_e)
    kv_p = kv_s = None
    rets_p, rets_s, avs = [], [], []
    for l in range(depth):
        lw = dict(
            big, norm_mix_w=norm_mix_w[l], a_norm_w=a_norm_w[l], a_ws=a_ws[l],
            a_bs_b=jnp.broadcast_to(a_bs[l][:, :, None], (A_GROUPS, A_BLOCK, A_GROUP)),
            b_norm_w=b_norm_w[l], c_qnorm_w=c_qnorm_w[l], c_knorm_w=c_knorm_w[l], c_lambda=c_lambda[l],
            c_subln_w=c_subln_w[l], norm_ffn_w=norm_ffn_w[l], w_rg=w_router_group[l], b_rg=b_router_group[l],
            w_re=w_router_expert[l], b_re=b_router_expert[l])
        yp, kv_p, rp, _ = _layer(yp, pos_p, l, depth, lw, None, None, kv_p)
        ys, kv_s, rn, avn = _layer(ys, pos_s, l, depth, lw, state_ret[l], (cache_k_c, cache_v_c), kv_s)
        rets_p.append(rp)
        rets_s.append(rn)
        avs.append(avn.reshape(ys.shape[0], ys.shape[1], A_WIDTH))
    as_cache = lambda buf, x: buf.reshape(depth, x.shape[0], x.shape[1], C_HEADS, LANES)
    return (yp, ys, as_cache(kv_p[0], x_prompt), as_cache(kv_p[1], x_prompt), jnp.stack(rets_p, 0),
            as_cache(kv_s[0], x_sample), as_cache(kv_s[1], x_sample), jnp.stack(rets_s, 0), jnp.stack(avs, 0))
```

```python
import functools
import math

import jax
import jax.numpy as jnp
from jax import lax
from jax.experimental import pallas as pl
from jax.experimental.pallas import tpu as pltpu

F32 = jnp.float32
BF16 = jnp.bfloat16

D_MODEL = 2048
CHUNK = 64
A_WIDTH = 1024
A_BLOCK = 128
A_GROUP = 128
A_GROUPS = 8
B_HEADS = 4
B_DK = 128
B_DV = 256
ROPE_BASE = 10000.0
C_HEADS = 8
C_DH = 64
C_DV = 128
N_GROUPS = 4
EXPERTS_PER_GROUP = 4
N_EXPERTS = 16
D_EXPERT = 512
EPS = 1e-6
D_IN = 14336

OFF_AU, OFF_AV, OFF_BQ, OFF_BK, OFF_BV, OFF_BG, OFF_CQ, OFF_CK, OFF_CV, OFF_GATE = (
    0, 1024, 2048, 2560, 3072, 4096, 5120, 6144, 7168, 8192)

LANES = 128
MIB = 1024 * 1024
NEG_BIG = -1e30
Q_SCALE = (C_DH ** -0.5) * math.log2(math.e)
VT_ROWS = C_DV + 16
HEADS_PER_STEP = 4


def _cparams(sem, vmem_mib):
    return pltpu.CompilerParams(dimension_semantics=sem, vmem_limit_bytes=vmem_mib * MIB)


def _rms(x, w):
    ms = jnp.mean(x * x, axis=-1, keepdims=True)
    return x * lax.rsqrt(ms + EPS) * w


def _pick(n, pref):
    t = min(pref, n)
    while n % t:
        t //= 2
    return t


def _inproj_kernel(x_ref, g_ref, w_ref, o_ref, xn_ref):
    @pl.when(pl.program_id(1) == 0)
    def _():
        xn_ref[...] = _rms(x_ref[...], g_ref[...]).astype(BF16)

    o_ref[...] = jnp.dot(xn_ref[...], w_ref[...], preferred_element_type=F32).astype(o_ref.dtype)


def _inproj(x2, g, w_bf, l):
    n = x2.shape[0]
    tm = _pick(n, 1024)
    tn = 1024
    return pl.pallas_call(
        _inproj_kernel,
        grid=(n // tm, D_IN // tn),
        in_specs=[pl.BlockSpec((tm, D_MODEL), lambda i, j: (i, 0)),
                  pl.BlockSpec((1, D_MODEL), lambda i, j: (0, 0)),
                  pl.BlockSpec((None, D_MODEL, tn), lambda i, j: (l, 0, j))],
        out_specs=pl.BlockSpec((tm, tn), lambda i, j: (i, j)),
        out_shape=jax.ShapeDtypeStruct((n, D_IN), BF16),
        scratch_shapes=[pltpu.VMEM((tm, D_MODEL), BF16)],
        compiler_params=_cparams(("parallel", "arbitrary"), 48),
    )(x2, g.reshape(1, D_MODEL), w_bf)


def _gmlp_kernel(u_ref, v_ref, nw_ref, ws_ref, bs_ref, y_ref, *rest, pb, nblk, emit_v):
    u = jax.nn.gelu(u_ref[...].astype(F32))
    v = _rms(jax.nn.gelu(v_ref[...].astype(F32)), nw_ref[...])
    if emit_v:
        rest[0][...] = v
    vb = v.astype(BF16)
    p = lax.broadcasted_iota(jnp.int32, (A_BLOCK, A_BLOCK), 0)
    q = lax.broadcasted_iota(jnp.int32, (A_BLOCK, A_BLOCK), 1)
    mask = (q // CHUNK) <= (p // CHUNK)
    for g in range(A_GROUPS):
        w = jnp.where(mask, ws_ref[g], 0.0).astype(BF16)
        bias = bs_ref[g, :pb, :]
        cs = slice(g * A_GROUP, (g + 1) * A_GROUP)
        for b in range(nblk):
            rs = slice(b * pb, (b + 1) * pb)
            vg = vb[rs, cs]
            if pb < A_BLOCK:
                vg = jnp.concatenate([vg, jnp.zeros((A_BLOCK - pb, A_GROUP), BF16)], axis=0)
            s = jnp.dot(w, vg, preferred_element_type=F32)[:pb] + bias
            y_ref[rs, cs] = (u[rs, cs] * s).astype(y_ref.dtype)


def _gmlp(proj2, seq_len, nw, ws, bs_b, emit_v):
    n = proj2.shape[0]
    if seq_len % A_BLOCK == 0:
        pb, nblk = A_BLOCK, 4 if seq_len % (4 * A_BLOCK) == 0 else 1
    else:
        assert seq_len <= A_BLOCK
        pb, nblk = seq_len, 1
    r = pb * nblk
    out_shape = [jax.ShapeDtypeStruct((n, A_WIDTH), BF16)]
    out_specs = [pl.BlockSpec((r, A_WIDTH), lambda i: (i, 0))]
    if emit_v:
        out_shape.append(jax.ShapeDtypeStruct((n, A_WIDTH), F32))
        out_specs.append(pl.BlockSpec((r, A_WIDTH), lambda i: (i, 0)))
    res = pl.pallas_call(
        functools.partial(_gmlp_kernel, pb=pb, nblk=nblk, emit_v=emit_v),
        grid=(n // r,),
        in_specs=[pl.BlockSpec((r, A_WIDTH), lambda i: (i, OFF_AU // A_WIDTH)),
                  pl.BlockSpec((r, A_WIDTH), lambda i: (i, OFF_AV // A_WIDTH)),
                  pl.BlockSpec((1, A_WIDTH), lambda i: (0, 0)),
                  pl.BlockSpec((A_GROUPS, A_BLOCK, A_BLOCK), lambda i: (0, 0, 0)),
                  pl.BlockSpec((A_GROUPS, A_BLOCK, A_GROUP), lambda i: (0, 0, 0))],
        out_specs=out_specs,
        out_shape=out_shape,
        compiler_params=_cparams(("parallel",), 32),
    )(proj2, proj2, nw.reshape(1, A_WIDTH), ws, bs_b)
    return res if emit_v else (res[0], None)


def _ret_kernel(q_ref, k_ref, v_ref, g_ref, cos_ref, sin_ref, dm_ref, qd_ref, kd_ref, sd_ref, nw_ref, *rest,
                has_init):
    if has_init:
        s0_ref, y_ref, sfin_ref, s_scr = rest
    else:
        y_ref, sfin_ref, s_scr = rest
    c = pl.program_id(1)

    @pl.when(c == 0)
    def _():
        s_scr[...] = s0_ref[...] if has_init else jnp.zeros_like(s_scr)

    cos = cos_ref[...]
    sin = sin_ref[...]
    nw = nw_ref[...]
    for h in range(B_HEADS):
        ks = slice(h * B_DK, (h + 1) * B_DK)
        vs = slice(h * B_DV, (h + 1) * B_DV)
        q = q_ref[:, ks].astype(F32)
        k = k_ref[:, ks].astype(F32)
        qr = q * cos + pltpu.roll(q, B_DK // 2, 1) * sin
        kr = (k * cos + pltpu.roll(k, B_DK // 2, 1) * sin) * (B_DK ** -0.5)
        qb = qr.astype(BF16)
        kb = kr.astype(BF16)
        v = v_ref[:, vs]
        attn = lax.dot_general(qb, kb, (((1,), (1,)), ((), ())), preferred_element_type=F32) * dm_ref[h]
        intra = jnp.dot(attn.astype(BF16), v, preferred_element_type=F32)
        s_prev = s_scr[h]
        inter = jnp.dot(qb, s_prev.astype(BF16), preferred_element_type=F32) * qd_ref[h]
        o = intra + inter
        kt = jnp.transpose(kr * kd_ref[h]).astype(BF16)
        s_scr[h] = sd_ref[h] * s_prev + jnp.dot(kt, v, preferred_element_type=F32)
        gate = g_ref[:, vs].astype(F32)
        y_ref[:, vs] = (_rms(o, nw) * (gate * jax.nn.sigmoid(gate))).astype(y_ref.dtype)

    @pl.when(c == pl.num_programs(1) - 1)
    def _():
        sfin_ref[...] = s_scr[...]


def _ret_tables(c):
    lg = jnp.log1p(-jnp.exp2(-5.0 - jnp.arange(B_HEADS, dtype=F32)))
    idx = jnp.arange(c, dtype=F32)
    diff = idx[:, None] - idx[None, :]
    dm = jnp.where(diff >= 0, jnp.exp(jnp.maximum(diff, 0.0)[None] * lg[:, None, None]), 0.0)
    qd = jnp.exp((idx + 1.0)[None, :] * lg[:, None])
    kd = jnp.exp((c - 1.0 - idx)[None, :] * lg[:, None])
    sd = jnp.exp(c * lg)
    return (dm, jnp.broadcast_to(qd[:, :, None], (B_HEADS, c, B_DV)),
            jnp.broadcast_to(kd[:, :, None], (B_HEADS, c, B_DK)),
            jnp.broadcast_to(sd[:, None, None], (B_HEADS, 1, B_DV)))


def _rope_tables(pos):
    half = B_DK // 2
    inv_freq = 1.0 / (ROPE_BASE ** jnp.linspace(0.0, 1.0, half, dtype=F32))
    ang = pos.astype(F32)[:, None] * inv_freq[None, :]
    cos, sin = jnp.cos(ang), jnp.sin(ang)
    return jnp.concatenate([cos, cos], axis=-1), jnp.concatenate([-sin, sin], axis=-1)


def _retention(proj3, pos, nw, s0):
    bsz, seq, _ = proj3.shape
    c = _pick(seq, 256)
    dm, qd, kd, sd = _ret_tables(c)
    cos, sin = _rope_tables(pos)
    has_init = s0 is not None
    qk_w = B_HEADS * B_DK
    v_w = B_HEADS * B_DV
    full = lambda shape: pl.BlockSpec(shape, lambda b, i: (0,) * len(shape))
    state = pl.BlockSpec((None, B_HEADS, B_DK, B_DV), lambda b, i: (b, 0, 0, 0))
    in_specs = [pl.BlockSpec((None, c, qk_w), lambda b, i: (b, i, OFF_BQ // qk_w)),
                pl.BlockSpec((None, c, qk_w), lambda b, i: (b, i, OFF_BK // qk_w)),
                pl.BlockSpec((None, c, v_w), lambda b, i: (b, i, OFF_BV // v_w)),
                pl.BlockSpec((None, c, v_w), lambda b, i: (b, i, OFF_BG // v_w)),
                pl.BlockSpec((c, B_DK), lambda b, i: (i, 0)),
                pl.BlockSpec((c, B_DK), lambda b, i: (i, 0)),
                full((B_HEADS, c, c)), full((B_HEADS, c, B_DV)), full((B_HEADS, c, B_DK)),
                full((B_HEADS, 1, B_DV)), full((1, B_DV))]
    args = [proj3, proj3, proj3, proj3, cos, sin, dm, qd, kd, sd, nw.reshape(1, B_DV)]
    if has_init:
        in_specs.append(state)
        args.append(s0)
    return pl.pallas_call(
        functools.partial(_ret_kernel, has_init=has_init),
        grid=(bsz, seq // c),
        in_specs=in_specs,
        out_specs=[pl.BlockSpec((None, c, v_w), lambda b, i: (b, i, 0)), state],
        out_shape=[jax.ShapeDtypeStruct((bsz, seq, v_w), BF16),
                   jax.ShapeDtypeStruct((bsz, B_HEADS, B_DK, B_DV), F32)],
        scratch_shapes=[pltpu.VMEM((B_HEADS, B_DK, B_DV), F32)],
        compiler_params=_cparams(("parallel", "arbitrary"), 32),
    )(*args)


def _group_mean_matrix():
    r = lax.broadcasted_iota(jnp.int32, (LANES, LANES), 0)
    c = lax.broadcasted_iota(jnp.int32, (LANES, LANES), 1)
    return jnp.where((r // C_DH) == (c // C_DH), 1.0 / C_DH, 0.0).astype(BF16)


def _cprep_kernel(q_ref, k_ref, v_ref, qw_ref, kw_ref, lam_ref, *rest, lam_init, emit_vt, n_prev, l):
    qn_ref, kn_ref, kf_ref, vf_ref, lamo_ref = rest[n_prev:n_prev + 5]
    rows = q_ref.shape[0]
    if n_prev == 0:
        for d in range(kf_ref.shape[0]):
            if d != l:
                kf_ref[d] = jnp.zeros(kf_ref.shape[1:], F32)
                vf_ref[d] = jnp.zeros(vf_ref.shape[1:], F32)
        kf_ref, vf_ref = kf_ref.at[l], vf_ref.at[l]
    gm = _group_mean_matrix()
    qw = qw_ref[...]
    kw = kw_ref[...]
    for h in range(C_HEADS):
        cs = slice(h * LANES, (h + 1) * LANES)
        hs = pl.ds(h, rows, stride=C_HEADS)
        x = q_ref[:, cs].astype(F32)
        ms = jnp.dot((x * x).astype(BF16), gm, preferred_element_type=F32)
        qn_ref[:, cs] = (x * lax.rsqrt(ms + EPS) * qw * Q_SCALE).astype(BF16)
        x = k_ref[:, cs].astype(F32)
        ms = jnp.dot((x * x).astype(BF16), gm, preferred_element_type=F32)
        kn = x * lax.rsqrt(ms + EPS) * kw
        kf_ref[hs, :] = kn
        kn_ref[:, cs] = kn.astype(BF16)
        vv = v_ref[:, cs].astype(F32)
        vf_ref[hs, :] = vv
        if emit_vt:
            vt_ref = rest[n_prev + 5]
            vt_ref[h, :C_DV, :] = jnp.transpose(vv).astype(BF16)
            vt_ref[h, C_DV:, :] = jnp.ones((VT_ROWS - C_DV, rows), BF16)
    lq = lam_ref[...]
    l01 = jnp.sum(lq[0:1] * lq[1:2], axis=-1, keepdims=True)
    l23 = jnp.sum(lq[2:3] * lq[3:4], axis=-1, keepdims=True)
    lam = jnp.exp(l01) - jnp.exp(l23) + lam_init
    lamo_ref[...] = jnp.broadcast_to(lam, lamo_ref.shape)


def _cprep(proj3, qw, kw, lam_p, lam_init, emit_vt, l, depth, prev_kv):
    bsz, seq, _ = proj3.shape
    r = _pick(seq, 512)
    w = C_HEADS * LANES
    qw2 = jnp.concatenate([qw, qw]).reshape(1, LANES)
    kw2 = jnp.concatenate([kw, kw]).reshape(1, LANES)
    blk = lambda off: pl.BlockSpec((None, r, w), lambda b, i: (b, i, off // w))
    row = pl.BlockSpec((None, r, w), lambda b, i: (b, i, 0))
    if prev_kv is None:
        kv = pl.BlockSpec((depth, None, r * C_HEADS, LANES), lambda b, i: (0, b, i, 0))
    else:
        kv = pl.BlockSpec((None, None, r * C_HEADS, LANES), lambda b, i: (l, b, i, 0))
    kv_shape = jax.ShapeDtypeStruct((depth, bsz, seq * C_HEADS, LANES), F32)
    out_shape = [jax.ShapeDtypeStruct((bsz, seq, w), BF16), jax.ShapeDtypeStruct((bsz, seq, w), BF16),
                 kv_shape, kv_shape, jax.ShapeDtypeStruct((8, LANES), F32)]
    out_specs = [row, row, kv, kv, pl.BlockSpec((8, LANES), lambda b, i: (0, 0))]
    if emit_vt:
        out_shape.append(jax.ShapeDtypeStruct((bsz, C_HEADS, VT_ROWS, seq), BF16))
        out_specs.append(pl.BlockSpec((None, C_HEADS, VT_ROWS, r), lambda b, i: (b, 0, 0, i)))
    in_specs = [blk(OFF_CQ), blk(OFF_CK), blk(OFF_CV),
                pl.BlockSpec((1, LANES), lambda b, i: (0, 0)),
                pl.BlockSpec((1, LANES), lambda b, i: (0, 0)),
                pl.BlockSpec((4, C_DH), lambda b, i: (0, 0))]
    args = [proj3, proj3, proj3, qw2, kw2, lam_p]
    aliases = {}
    if prev_kv is not None:
        in_specs += [pl.BlockSpec(memory_space=pl.ANY)] * 2
        aliases = {len(args): 2, len(args) + 1: 3}
        args += list(prev_kv)
    return pl.pallas_call(
        functools.partial(_cprep_kernel, lam_init=lam_init, emit_vt=emit_vt, n_prev=len(aliases), l=l),
        grid=(bsz, seq // r),
        in_specs=in_specs,
        out_specs=out_specs,
        out_shape=out_shape,
        input_output_aliases=aliases,
        compiler_params=_cparams(("arbitrary", "arbitrary"), 40),
    )(*args)


def _dattn_kernel(q_ref, k_ref, vt_ref, lam_ref, sw_ref, o_ref, s_a, s_b, m_scr, a_scr, *, t, out_scale):
    qi = pl.program_id(2)
    lane = lax.broadcasted_iota(jnp.int32, (t, LANES), 1)
    qc = []
    for hh in range(HEADS_PER_STEP):
        q = q_ref[:, hh * LANES:(hh + 1) * LANES]
        qc += [jnp.where(lane < C_DH, q, jnp.zeros_like(q)), jnp.where(lane >= C_DH, q, jnp.zeros_like(q))]
    m_scr[...] = jnp.full(m_scr.shape, NEG_BIG, F32)
    a_scr[...] = jnp.zeros(a_scr.shape, F32)

    def scores(blk, s_ref):
        rows = pl.ds(pl.multiple_of(blk * t, t), t)
        for ch in range(2 * HEADS_PER_STEP):
            hh = ch // 2
            kb = k_ref[rows, hh * LANES:(hh + 1) * LANES]
            s_ref[ch] = lax.dot_general(kb, qc[ch], (((1,), (1,)), ((), ())), preferred_element_type=F32)

    def update(blk, s_ref, masked=False):
        cols = pl.ds(pl.multiple_of(blk * t, t), t)
        if masked:
            kpos = lax.broadcasted_iota(jnp.int32, (t, t), 0)
            qpos = lax.broadcasted_iota(jnp.int32, (t, t), 1)
            vis = (kpos // CHUNK) <= (qpos // CHUNK)
        for ch in range(2 * HEADS_PER_STEP):
            s = s_ref[ch]
            if masked:
                s = jnp.where(vis, s, NEG_BIG)
            m_old = m_scr[ch]
            m_new = jnp.maximum(m_old, jnp.max(s, axis=0, keepdims=True))
            alpha = jnp.exp2(m_old - m_new)
            p = jnp.exp2(s - m_new).astype(BF16)
            a_scr[ch] = alpha * a_scr[ch] + jnp.dot(vt_ref[ch // 2, :, cols], p, preferred_element_type=F32)
            m_scr[ch] = m_new

    scores(qi, s_a)

    @pl.when(qi == 0)
    def _():
        update(qi, s_a, masked=True)

    @pl.when(qi > 0)
    def _():
        scores(0, s_b)
        update(qi, s_a, masked=True)

        def pair(jj, carry):
            j = 2 * jj
            scores(j + 1, s_a)
            update(j, s_b)
            scores(j + 2, s_b)
            update(j + 1, s_a)
            return carry

        lax.fori_loop(0, (qi - 1) // 2, pair, 0)

        @pl.when(qi % 2 == 1)
        def _():
            update(qi - 1, s_b)

        @pl.when(qi % 2 == 0)
        def _():
            scores(qi - 1, s_a)
            update(qi - 2, s_b)
            update(qi - 1, s_a)

    lam = lam_ref[0:1, 0:1]
    for hh in range(HEADS_PER_STEP):
        c0, c1 = 2 * hh, 2 * hh + 1
        o_t = (a_scr[c0, :C_DV, :] / a_scr[c0, C_DV:C_DV + 1, :]
               - lam * (a_scr[c1, :C_DV, :] / a_scr[c1, C_DV:C_DV + 1, :]))
        ms = jnp.mean(o_t * o_t, axis=0, keepdims=True)
        o = jnp.transpose(o_t * lax.rsqrt(ms + EPS))
        o_ref[:, hh * C_DV:(hh + 1) * C_DV] = (o * sw_ref[...] * out_scale).astype(o_ref.dtype)


def _dattn_prompt(qn, kn, vt, lam_b, sw, out_scale):
    bsz, seq, w = qn.shape
    t = _pick(seq, 512)
    hw = HEADS_PER_STEP * LANES
    chains = 2 * HEADS_PER_STEP
    return pl.pallas_call(
        functools.partial(_dattn_kernel, t=t, out_scale=out_scale),
        grid=(bsz, C_HEADS // HEADS_PER_STEP, seq // t),
        in_specs=[pl.BlockSpec((None, t, hw), lambda b, h, i: (b, i, h)),
                  pl.BlockSpec((None, seq, hw), lambda b, h, i: (b, 0, h), pipeline_mode=pl.Buffered(1)),
                  pl.BlockSpec((None, HEADS_PER_STEP, VT_ROWS, seq), lambda b, h, i: (b, h, 0, 0),
                               pipeline_mode=pl.Buffered(1)),
                  pl.BlockSpec((8, LANES), lambda b, h, i: (0, 0)),
                  pl.BlockSpec((1, C_DV), lambda b, h, i: (0, 0))],
        out_specs=pl.BlockSpec((None, t, hw), lambda b, h, i: (b, i, h)),
        out_shape=jax.ShapeDtypeStruct((bsz, seq, w), BF16),
        scratch_shapes=[pltpu.VMEM((chains, t, t), F32), pltpu.VMEM((chains, t, t), F32),
                        pltpu.VMEM((chains, 1, t), F32), pltpu.VMEM((chains, VT_ROWS, t), F32)],
        compiler_params=_cparams(("parallel", "parallel", "arbitrary"), 52),
    )(qn, kn, vt, lam_b, sw.reshape(1, C_DV))


def _dattn_sample_kernel(q_ref, kc_ref, vc_ref, kn_ref, vn_ref, lam_ref, sw_ref, o_ref, *, past, out_scale):
    lq = q_ref.shape[0]
    lane = lax.broadcasted_iota(jnp.int32, (lq, LANES), 1)
    qpos_c = past + lax.broadcasted_iota(jnp.int32, (lq, past), 0)
    kpos_c = lax.broadcasted_iota(jnp.int32, (lq, past), 1)
    vis_c = (kpos_c // CHUNK) <= (qpos_c // CHUNK)
    qpos_n = past + lax.broadcasted_iota(jnp.int32, (lq, lq), 0)
    kpos_n = past + lax.broadcasted_iota(jnp.int32, (lq, lq), 1)
    vis_n = (kpos_n // CHUNK) <= (qpos_n // CHUNK)
    lam = lam_ref[0:1, 0:1]
    nt = (((1,), (1,)), ((), ()))
    for h in range(C_HEADS):
        cs = slice(h * LANES, (h + 1) * LANES)
        q = q_ref[:, cs]
        kc = kc_ref[pl.ds(h, past, stride=C_HEADS), :].astype(BF16)
        vc = vc_ref[pl.ds(h, past, stride=C_HEADS), :].astype(BF16)
        kn = kn_ref[:, cs]
        vn = vn_ref[:, cs]
        outs = []
        for c in range(2):
            qc = jnp.where((lane // C_DH) == c, q, jnp.zeros_like(q))
            s_c = jnp.where(vis_c, lax.dot_general(qc, kc, nt, preferred_element_type=F32), NEG_BIG)
            s_n = jnp.where(vis_n, lax.dot_general(qc, kn, nt, preferred_element_type=F32), NEG_BIG)
            m = jnp.maximum(jnp.max(s_c, axis=-1, keepdims=True), jnp.max(s_n, axis=-1, keepdims=True))
            p_c = jnp.exp2(s_c - m)
            p_n = jnp.exp2(s_n - m)
            l = jnp.sum(p_c, axis=-1, keepdims=True) + jnp.sum(p_n, axis=-1, keepdims=True)
            acc = (jnp.dot(p_c.astype(BF16), vc, preferred_element_type=F32)
                   + jnp.dot(p_n.astype(BF16), vn, preferred_element_type=F32))
            outs.append(acc / l)
        o = outs[0] - lam * outs[1]
        o_ref[:, cs] = (_rms(o, sw_ref[...]) * out_scale).astype(o_ref.dtype)


def _dattn_sample(qn, kn, proj3, cache_k, cache_v, l, lam_b, sw, out_scale):
    bsz, lq, w = qn.shape
    depth, _, past = cache_k.shape[:3]
    ck = cache_k.reshape(depth, bsz, past * C_HEADS, LANES)
    cv = cache_v.reshape(depth, bsz, past * C_HEADS, LANES)
    rows = lambda r: pl.BlockSpec((None, r, w), lambda b: (b, 0, 0))
    cache = pl.BlockSpec((None, None, past * C_HEADS, LANES), lambda b: (l, b, 0, 0))
    return pl.pallas_call(
        functools.partial(_dattn_sample_kernel, past=past, out_scale=out_scale),
        grid=(bsz,),
        in_specs=[rows(lq), cache, cache, rows(lq),
                  pl.BlockSpec((None, lq, w), lambda b: (b, 0, OFF_CV // w)),
                  pl.BlockSpec((8, LANES), lambda b: (0, 0)),
                  pl.BlockSpec((1, C_DV), lambda b: (0, 0))],
        out_specs=rows(lq),
        out_shape=jax.ShapeDtypeStruct((bsz, lq, w), BF16),
        compiler_params=_cparams(("parallel",), 40),
    )(qn, ck, cv, kn, proj3, lam_b, sw.reshape(1, C_DV))


def _merge_kernel(ya_ref, yb_ref, yc_ref, ga_ref, gb_ref, gc_ref, w_ref, o_ref):
    acc = None
    for y_ref, g_ref, n in ((ya_ref, ga_ref, 0), (yb_ref, gb_ref, 1), (yc_ref, gc_ref, 2)):
        up = jnp.dot(y_ref[...], w_ref[n], preferred_element_type=F32)
        term = jax.nn.sigmoid(g_ref[...].astype(F32)) * up
        acc = term if acc is None else acc + term
    o_ref[...] = acc.astype(o_ref.dtype)


def _merge(ya, yb, yc, proj2, wb_bf, l):
    n = ya.shape[0]
    tm = _pick(n, 512)
    tn = 1024
    ybs = pl.BlockSpec((tm, A_WIDTH), lambda j, i: (i, 0))
    gate = lambda b: pl.BlockSpec((tm, tn), lambda j, i: (i, (OFF_GATE + b * D_MODEL) // tn + j))
    return pl.pallas_call(
        _merge_kernel,
        grid=(D_MODEL // tn, n // tm),
        in_specs=[ybs, ybs, ybs, gate(0), gate(1), gate(2),
                  pl.BlockSpec((None, 3, A_WIDTH, tn), lambda j, i: (l, 0, 0, j))],
        out_specs=pl.BlockSpec((tm, tn), lambda j, i: (i, j)),
        out_shape=jax.ShapeDtypeStruct((n, D_MODEL), BF16),
        compiler_params=_cparams(("parallel", "parallel"), 48),
    )(ya, yb, yc, proj2, proj2, proj2, wb_bf)


ROUTE_GROUP_LANE = N_EXPERTS


def _outproj_router_kernel(u_ref, w_ref, x_ref, nw_ref, wcat_ref, br_ref, xo_ref, o_ref, ot_ref, cnt_ref, run_scr):
    @pl.when(pl.program_id(0) == 0)
    def _():
        run_scr[...] = jnp.zeros_like(run_scr)

    tm = x_ref.shape[0]
    x_new = x_ref[...] + jnp.dot(u_ref[...], w_ref[...], preferred_element_type=F32)
    xo_ref[...] = x_new
    xn = _rms(x_new, nw_ref[...])
    hi = xn.astype(BF16)
    lo = (xn - hi.astype(F32)).astype(BF16)
    both = jnp.dot(hi, wcat_ref[...], preferred_element_type=F32)
    logit = (both[:, :LANES] + both[:, LANES:]
             + jnp.dot(lo, wcat_ref[:, :LANES], preferred_element_type=F32) + br_ref[...])
    lane = lax.broadcasted_iota(jnp.int32, logit.shape, 1).astype(F32)
    big = 1e9
    is_g = (lane >= ROUTE_GROUP_LANE) & (lane < ROUTE_GROUP_LANE + N_GROUPS)
    lg = jnp.where(is_g, logit, NEG_BIG)
    mg = jnp.max(lg, axis=-1, keepdims=True)
    g_sel = jnp.min(jnp.where(lg == mg, lane - ROUTE_GROUP_LANE, big), axis=-1, keepdims=True)
    p_group = 1.0 / jnp.sum(jnp.exp(lg - mg), axis=-1, keepdims=True)
    lo_lane = g_sel * EXPERTS_PER_GROUP
    in_g = (lane >= lo_lane) & (lane < lo_lane + EXPERTS_PER_GROUP)
    le = jnp.where(in_g, logit, NEG_BIG)
    v1 = jnp.max(le, axis=-1, keepdims=True)
    i1 = jnp.min(jnp.where(le == v1, lane, big), axis=-1, keepdims=True)
    le2 = jnp.where(lane == i1, NEG_BIG, le)
    v2 = jnp.max(le2, axis=-1, keepdims=True)
    i2 = jnp.min(jnp.where(le2 == v2, lane, big), axis=-1, keepdims=True)
    e2 = jnp.exp(v2 - v1)
    den = 1.0 + e2
    p1 = p_group / den
    p2 = p_group * e2 / den
    oh1 = jnp.where(lane == i1, 1.0, 0.0)
    oh2 = jnp.where(lane == i2, 1.0, 0.0)
    oh = oh1 + oh2
    row = lax.broadcasted_iota(jnp.int32, (tm, tm), 0)
    col = lax.broadcasted_iota(jnp.int32, (tm, tm), 1)
    tri = jnp.where(row > col, 1.0, 0.0).astype(BF16)
    before = run_scr[...] + jnp.dot(tri, oh.astype(BF16), preferred_element_type=F32)
    r1 = jnp.sum(oh1 * before, axis=-1, keepdims=True)
    r2 = jnp.sum(oh2 * before, axis=-1, keepdims=True)
    run = run_scr[...] + jnp.sum(oh, axis=0, keepdims=True)
    run_scr[...] = run
    cnt_ref[...] = jnp.broadcast_to(run, cnt_ref.shape)
    out = jnp.zeros_like(logit)
    for n, val in enumerate((i1, i2, p1, p2, r1, r2)):
        out = jnp.where(lane == n, val, out)
    o_ref[...] = out
    ot_ref[...] = jnp.transpose(out)[:8]


def _outproj_router(u, w_bf, x2, l, nw, w_rg, b_rg, w_re, b_re):
    n = x2.shape[0]
    tm = _pick(n, 512)
    w = jnp.zeros((D_MODEL, LANES), F32).at[:, :N_EXPERTS].set(w_re).at[:, N_EXPERTS:N_EXPERTS + N_GROUPS].set(w_rg)
    br = jnp.zeros((1, LANES), F32).at[0, :N_EXPERTS].set(b_re).at[0, N_EXPERTS:N_EXPERTS + N_GROUPS].set(b_rg)
    whi = w.astype(BF16)
    wlo = (w - whi.astype(F32)).astype(BF16)
    full = lambda shape: pl.BlockSpec(shape, lambda i: (0, 0))
    rows = pl.BlockSpec((tm, D_MODEL), lambda i: (i, 0))
    return pl.pallas_call(
        _outproj_router_kernel,
        grid=(n // tm,),
        in_specs=[rows, pl.BlockSpec((None, D_MODEL, D_MODEL), lambda i: (l, 0, 0)), rows,
                  full((1, D_MODEL)), full((D_MODEL, 2 * LANES)), full((1, LANES))],
        out_specs=[rows, pl.BlockSpec((tm, LANES), lambda i: (i, 0)), pl.BlockSpec((8, tm), lambda i: (0, i)),
                   full((8, LANES))],
        out_shape=[jax.ShapeDtypeStruct((n, D_MODEL), F32), jax.ShapeDtypeStruct((n, LANES), F32),
                   jax.ShapeDtypeStruct((8, n), F32), jax.ShapeDtypeStruct((8, LANES), F32)],
        scratch_shapes=[pltpu.VMEM((1, LANES), F32)],
        compiler_params=_cparams(("arbitrary",), 48),
    )(u, w_bf, x2, nw.reshape(1, D_MODEL), jnp.concatenate([whi, wlo], axis=1), br)


def _moe_plan(e1, e2, r1, r2, cnt, tm_e):
    n = e1.shape[0]
    ar = jnp.arange(N_EXPERTS, dtype=jnp.int32)
    offs = jnp.cumsum(cnt) - cnt
    dest = jnp.stack([jnp.take(offs, e1) + r1, jnp.take(offs, e2) + r2])
    n_tiles = (2 * n) // tm_e
    first = offs // tm_e
    last = jnp.where(cnt > 0, (offs + cnt - 1) // tm_e, first - 1)
    n_items = last - first + 1
    item_end = jnp.cumsum(n_items)
    item_start = item_end - n_items
    n_work = n_tiles + N_EXPERTS - 1
    w = jnp.arange(n_work, dtype=jnp.int32)
    e_w = jnp.minimum(jnp.sum((w[:, None] >= item_end[None, :]).astype(jnp.int32), axis=1), N_EXPERTS - 1)
    valid = w < item_end[-1]
    ohw = (e_w[:, None] == ar[None, :]).astype(jnp.int32)
    sel = lambda tab: jnp.sum(ohw * tab[None, :], axis=1)
    tile = sel(first) + (w - sel(item_start))
    lo = jnp.maximum(sel(offs), tile * tm_e)
    hi = jnp.minimum(sel(offs) + sel(cnt), (tile + 1) * tm_e)
    last_e = jnp.max(jnp.where(n_items > 0, ar, 0))
    tile = jnp.where(valid, tile, n_tiles - 1)
    e_w = jnp.where(valid, e_w, last_e)
    lo = jnp.where(valid, lo, 0)
    hi = jnp.where(valid, hi, 0)
    return dest.astype(jnp.int32), tile.astype(jnp.int32), e_w.astype(jnp.int32), lo.astype(jnp.int32), hi.astype(jnp.int32)


def _row_copy(src_ref, src_row, dst_ref, dst_row, sem):
    return pltpu.make_async_copy(src_ref.at[pl.ds(src_row, 1), :], dst_ref.at[pl.ds(dst_row, 1), :], sem)


ROW_UNROLL = 8


def _dispatch_kernel(dest_ref, x_ref, xs_ref, sem, *, tm):
    def copies(i):
        for u in range(ROW_UNROLL):
            r = i * ROW_UNROLL + u
            for k in range(2):
                yield _row_copy(x_ref, r, xs_ref, dest_ref[k, r], sem)

    def issue(i, carry):
        for cp in copies(i):
            cp.start()
        return carry

    def drain(i, carry):
        for cp in copies(i):
            cp.wait()
        return carry

    lax.fori_loop(0, tm // ROW_UNROLL, issue, 0)
    lax.fori_loop(0, tm // ROW_UNROLL, drain, 0)


def _dispatch(x2, dest3, tm):
    n = x2.shape[0]
    return pl.pallas_call(
        functools.partial(_dispatch_kernel, tm=tm),
        grid=(n // tm,),
        in_specs=[pl.BlockSpec((None, 2, tm), lambda i: (i, 0, 0), memory_space=pltpu.SMEM),
                  pl.BlockSpec((tm, D_MODEL), lambda i: (i, 0))],
        out_specs=pl.BlockSpec(memory_space=pl.ANY),
        out_shape=jax.ShapeDtypeStruct((2 * n, D_MODEL), F32),
        scratch_shapes=[pltpu.SemaphoreType.DMA(())],
        compiler_params=_cparams(("arbitrary",), 32),
    )(dest3, x2)


def _experts_kernel(tile_ref, exp_ref, lo_ref, hi_ref, xs_ref, nw_ref, wg_ref, wu_ref, wd_ref, ys_ref, *, tm):
    w = pl.program_id(0)
    lo = lo_ref[w]
    hi = hi_ref[w]
    tile = tile_ref[w]
    first_visit = jnp.logical_or(w == 0, tile_ref[jnp.maximum(w - 1, 0)] != tile)

    @pl.when(hi > lo)
    def _():
        xn = _rms(xs_ref[...], nw_ref[...]).astype(BF16)
        gate = jnp.dot(xn, wg_ref[...].astype(BF16), preferred_element_type=F32)
        up = jnp.dot(xn, wu_ref[...].astype(BF16), preferred_element_type=F32)
        h = (gate * jax.nn.sigmoid(gate) * up).astype(BF16)
        y = jnp.dot(h, wd_ref[...].astype(BF16), preferred_element_type=F32)

        @pl.when(first_visit)
        def _():
            ys_ref[...] = y

        @pl.when(jnp.logical_not(first_visit))
        def _():
            rows = tile * tm + lax.broadcasted_iota(jnp.int32, (tm, 1), 0)
            ys_ref[...] = jnp.where((rows >= lo) & (rows < hi), y, ys_ref[...])


def _experts(xs, nw, wg_bf, wu_bf, wd_bf, l, tile, exp, lo, hi, tm):
    n_work = tile.shape[0]
    grid_spec = pltpu.PrefetchScalarGridSpec(
        num_scalar_prefetch=4,
        grid=(n_work,),
        in_specs=[pl.BlockSpec((tm, D_MODEL), lambda w, t, e, lo, hi: (t[w], 0)),
                  pl.BlockSpec((1, D_MODEL), lambda w, t, e, lo, hi: (0, 0)),
                  pl.BlockSpec((None, None, D_MODEL, D_EXPERT), lambda w, t, e, lo, hi: (l, e[w], 0, 0)),
                  pl.BlockSpec((None, None, D_MODEL, D_EXPERT), lambda w, t, e, lo, hi: (l, e[w], 0, 0)),
                  pl.BlockSpec((None, None, D_EXPERT, D_MODEL), lambda w, t, e, lo, hi: (l, e[w], 0, 0))],
        out_specs=pl.BlockSpec((tm, D_MODEL), lambda w, t, e, lo, hi: (t[w], 0)),
    )
    return pl.pallas_call(
        functools.partial(_experts_kernel, tm=tm),
        grid_spec=grid_spec,
        out_shape=jax.ShapeDtypeStruct(xs.shape, F32),
        compiler_params=_cparams(("arbitrary",), 56),
    )(tile, exp, lo, hi, xs, nw.reshape(1, D_MODEL), wg_bf, wu_bf, wd_bf)


def _combine_kernel(dest_ref, route_ref, x_ref, ys_ref, o_ref, buf0, buf1, sems, *, tm, n_tok):
    i = pl.program_id(0)

    def copies(step, slot, j):
        for u in range(ROW_UNROLL):
            r = j * ROW_UNROLL + u
            tok = step * tm + r
            yield _row_copy(ys_ref, dest_ref[tok], buf0.at[slot], r, sems.at[slot])
            yield _row_copy(ys_ref, dest_ref[n_tok + tok], buf1.at[slot], r, sems.at[slot])

    def issue(step, slot):
        def body(j, carry):
            for cp in copies(step, slot, j):
                cp.start()
            return carry

        lax.fori_loop(0, tm // ROW_UNROLL, body, 0)

    def drain(step, slot):
        def body(j, carry):
            for cp in copies(step, slot, j):
                cp.wait()
            return carry

        lax.fori_loop(0, tm // ROW_UNROLL, body, 0)

    slot = i % 2

    @pl.when(i == 0)
    def _():
        issue(0, 0)

    @pl.when(i + 1 < pl.num_programs(0))
    def _():
        issue(i + 1, 1 - slot)

    drain(i, slot)
    route = route_ref[...]
    o_ref[...] = x_ref[...] + route[:, 2:3] * buf0[slot] + route[:, 3:4] * buf1[slot]


def _combine(x2, route, ys, dest, tm):
    n = x2.shape[0]
    grid_spec = pltpu.PrefetchScalarGridSpec(
        num_scalar_prefetch=1,
        grid=(n // tm,),
        in_specs=[pl.BlockSpec((tm, LANES), lambda i, d: (i, 0)),
                  pl.BlockSpec((tm, D_MODEL), lambda i, d: (i, 0)),
                  pl.BlockSpec(memory_space=pl.ANY)],
        out_specs=pl.BlockSpec((tm, D_MODEL), lambda i, d: (i, 0)),
        scratch_shapes=[pltpu.VMEM((2, tm, D_MODEL), F32), pltpu.VMEM((2, tm, D_MODEL), F32),
                        pltpu.SemaphoreType.DMA((2,))],
    )
    return pl.pallas_call(
        functools.partial(_combine_kernel, tm=tm, n_tok=n),
        grid_spec=grid_spec,
        out_shape=jax.ShapeDtypeStruct((n, D_MODEL), F32),
        compiler_params=_cparams(("arbitrary",), 32),
    )(dest.reshape(2 * n), route, x2, ys)


def _ffn(x2, route, route_t, cnt, lw, l):
    n = x2.shape[0]
    fields = route_t.astype(jnp.int32)
    tm_e = _pick(2 * n, 512) if n >= 4096 else 128
    tm_r = _pick(n, 256)
    dest, tile, exp, lo, hi = _moe_plan(fields[0], fields[1], fields[4], fields[5],
                                        cnt[0, :N_EXPERTS].astype(jnp.int32), tm_e)
    dest3 = dest.reshape(2, n // tm_r, tm_r).transpose(1, 0, 2)
    xs = _dispatch(x2, dest3, tm_r)
    ys = _experts(xs, lw["norm_ffn_w"], lw["wg"], lw["wu"], lw["wd"], l, tile, exp, lo, hi, tm_e)
    return _combine(x2, route, ys, dest, tm_r)


def _layer(x3, pos, l, depth, lw, ret_state, kv_cache, prev_kv):
    bsz, seq, _ = x3.shape
    n = bsz * seq
    sample = ret_state is not None
    lam_init = 0.8 - 0.6 * math.exp(-0.3 * l)
    x2 = x3.reshape(n, D_MODEL)
    proj2 = _inproj(x2, lw["norm_mix_w"], lw["w_in"], l)
    proj3 = proj2.reshape(bsz, seq, D_IN)

    a_y, a_v = _gmlp(proj2, seq, lw["a_norm_w"], lw["a_ws"], lw["a_bs_b"], emit_v=sample)
    b_y, s_new = _retention(proj3, pos, lw["b_norm_w"], ret_state)
    prep = _cprep(proj3, lw["c_qnorm_w"], lw["c_knorm_w"], lw["c_lambda"], lam_init, not sample, l, depth, prev_kv)
    qn, kn, kf_all, vf_all, lam_b = prep[:5]
    if sample:
        c_y = _dattn_sample(qn, kn, proj3, kv_cache[0], kv_cache[1], l, lam_b, lw["c_subln_w"], 1.0 - lam_init)
    else:
        c_y = _dattn_prompt(qn, kn, prep[5], lam_b, lw["c_subln_w"], 1.0 - lam_init)

    u = _merge(a_y, b_y.reshape(n, A_WIDTH), c_y.reshape(n, A_WIDTH), proj2, lw["w_branch"], l)
    x2, route, route_t, cnt = _outproj_router(u, lw["w_out"], x2, l, lw["norm_ffn_w"], lw["w_rg"], lw["b_rg"],
                                              lw["w_re"], lw["b_re"])
    x2 = _ffn(x2, route, route_t, cnt, lw, l)
    return x2.reshape(bsz, seq, D_MODEL), (kf_all, vf_all), s_new, a_v


def kernel(x_prompt, x_sample, cache_k_c, cache_v_c, state_ret, norm_mix_w, w_in, a_norm_w, a_ws, a_bs, b_norm_w, c_qnorm_w, c_knorm_w, c_lambda, c_subln_w, w_branch, w_out, norm_ffn_w, w_router_group, b_router_group, w_router_expert, b_router_expert, w_gate_e, w_up_e, w_down_e):
    depth = w_in.shape[0]
    past = cache_k_c.shape[2]
    pos_p = jnp.arange(x_prompt.shape[1])
    pos_s = past + jnp.arange(x_sample.shape[1])
    yp, ys = x_prompt, x_sample
    big = dict(w_in=w_in.astype(BF16), w_branch=w_branch.astype(BF16), w_out=w_out.astype(BF16),
               wg=w_gate_e, wu=w_up_e, wd=w_down_e)
    kv_p = kv_s = None
    rets_p, rets_s, avs = [], [], []
    for l in range(depth):
        lw = dict(
            big, norm_mix_w=norm_mix_w[l], a_norm_w=a_norm_w[l], a_ws=a_ws[l],
            a_bs_b=jnp.broadcast_to(a_bs[l][:, :, None], (A_GROUPS, A_BLOCK, A_GROUP)),
            b_norm_w=b_norm_w[l], c_qnorm_w=c_qnorm_w[l], c_knorm_w=c_knorm_w[l], c_lambda=c_lambda[l],
            c_subln_w=c_subln_w[l], norm_ffn_w=norm_ffn_w[l], w_rg=w_router_group[l], b_rg=b_router_group[l],
            w_re=w_router_expert[l], b_re=b_router_expert[l])
        yp, kv_p, rp, _ = _layer(yp, pos_p, l, depth, lw, None, None, kv_p)
        ys, kv_s, rn, avn = _layer(ys, pos_s, l, depth, lw, state_ret[l], (cache_k_c, cache_v_c), kv_s)
        rets_p.append(rp)
        rets_s.append(rn)
        avs.append(avn.reshape(ys.shape[0], ys.shape[1], A_WIDTH))
    as_cache = lambda buf, x: buf.reshape(depth, x.shape[0], x.shape[1], C_HEADS, LANES)
    return (yp, ys, as_cache(kv_p[0], x_prompt), as_cache(kv_p[1], x_prompt), jnp.stack(rets_p, 0),
            as_cache(kv_s[0], x_sample), as_cache(kv_s[1], x_sample), jnp.stack(rets_s, 0), jnp.stack(avs, 0))
```

```python
import functools
import math

import jax
import jax.numpy as jnp
from jax import lax
from jax.experimental import pallas as pl
from jax.experimental.pallas import tpu as pltpu

F32 = jnp.float32
BF16 = jnp.bfloat16

D_MODEL = 2048
CHUNK = 64
A_WIDTH = 1024
A_BLOCK = 128
A_GROUP = 128
A_GROUPS = 8
B_HEADS = 4
B_DK = 128
B_DV = 256
ROPE_BASE = 10000.0
C_HEADS = 8
C_DH = 64
C_DV = 128
N_GROUPS = 4
EXPERTS_PER_GROUP = 4
N_EXPERTS = 16
D_EXPERT = 512
EPS = 1e-6
D_IN = 14336

OFF_AU, OFF_AV, OFF_BQ, OFF_BK, OFF_BV, OFF_BG, OFF_CQ, OFF_CK, OFF_CV, OFF_GATE = (
    0, 1024, 2048, 2560, 3072, 4096, 5120, 6144, 7168, 8192)

LANES = 128
MIB = 1024 * 1024
NEG_BIG = -1e30
Q_SCALE = (C_DH ** -0.5) * math.log2(math.e)
VT_ROWS = C_DV + 16
HEADS_PER_STEP = 4


def _cparams(sem, vmem_mib):
    return pltpu.CompilerParams(dimension_semantics=sem, vmem_limit_bytes=vmem_mib * MIB)


def _rms(x, w):
    ms = jnp.mean(x * x, axis=-1, keepdims=True)
    return x * lax.rsqrt(ms + EPS) * w


def _pick(n, pref):
    t = min(pref, n)
    while n % t:
        t //= 2
    return t


def _inproj_kernel(x_ref, g_ref, w_ref, o_ref, xn_ref):
    @pl.when(pl.program_id(1) == 0)
    def _():
        xn_ref[...] = _rms(x_ref[...], g_ref[...]).astype(BF16)

    o_ref[...] = jnp.dot(xn_ref[...], w_ref[...], preferred_element_type=F32).astype(o_ref.dtype)


def _inproj(x2, g, w_bf, l):
    n = x2.shape[0]
    tm = _pick(n, 1024)
    tn = 1024
    return pl.pallas_call(
        _inproj_kernel,
        grid=(n // tm, D_IN // tn),
        in_specs=[pl.BlockSpec((tm, D_MODEL), lambda i, j: (i, 0)),
                  pl.BlockSpec((1, D_MODEL), lambda i, j: (0, 0)),
                  pl.BlockSpec((None, D_MODEL, tn), lambda i, j: (l, 0, j))],
        out_specs=pl.BlockSpec((tm, tn), lambda i, j: (i, j)),
        out_shape=jax.ShapeDtypeStruct((n, D_IN), BF16),
        scratch_shapes=[pltpu.VMEM((tm, D_MODEL), BF16)],
        compiler_params=_cparams(("parallel", "arbitrary"), 48),
    )(x2, g.reshape(1, D_MODEL), w_bf)


def _gmlp_kernel(u_ref, v_ref, nw_ref, ws_ref, bs_ref, y_ref, *rest, pb, nblk, emit_v):
    u = jax.nn.gelu(u_ref[...].astype(F32))
    v = _rms(jax.nn.gelu(v_ref[...].astype(F32)), nw_ref[...])
    if emit_v:
        rest[0][...] = v
    vb = v.astype(BF16)
    p = lax.broadcasted_iota(jnp.int32, (A_BLOCK, A_BLOCK), 0)
    q = lax.broadcasted_iota(jnp.int32, (A_BLOCK, A_BLOCK), 1)
    mask = (q // CHUNK) <= (p // CHUNK)
    for g in range(A_GROUPS):
        w = jnp.where(mask, ws_ref[g], 0.0).astype(BF16)
        bias = bs_ref[g, :pb, :]
        cs = slice(g * A_GROUP, (g + 1) * A_GROUP)
        for b in range(nblk):
            rs = slice(b * pb, (b + 1) * pb)
            vg = vb[rs, cs]
            if pb < A_BLOCK:
                vg = jnp.concatenate([vg, jnp.zeros((A_BLOCK - pb, A_GROUP), BF16)], axis=0)
            s = jnp.dot(w, vg, preferred_element_type=F32)[:pb] + bias
            y_ref[rs, cs] = (u[rs, cs] * s).astype(y_ref.dtype)


def _gmlp(proj2, seq_len, nw, ws, bs_b, emit_v):
    n = proj2.shape[0]
    if seq_len % A_BLOCK == 0:
        pb, nblk = A_BLOCK, 4 if seq_len % (4 * A_BLOCK) == 0 else 1
    else:
        assert seq_len <= A_BLOCK
        pb, nblk = seq_len, 1
    r = pb * nblk
    out_shape = [jax.ShapeDtypeStruct((n, A_WIDTH), BF16)]
    out_specs = [pl.BlockSpec((r, A_WIDTH), lambda i: (i, 0))]
    if emit_v:
        out_shape.append(jax.ShapeDtypeStruct((n, A_WIDTH), F32))
        out_specs.append(pl.BlockSpec((r, A_WIDTH), lambda i: (i, 0)))
    res = pl.pallas_call(
        functools.partial(_gmlp_kernel, pb=pb, nblk=nblk, emit_v=emit_v),
        grid=(n // r,),
        in_specs=[pl.BlockSpec((r, A_WIDTH), lambda i: (i, OFF_AU // A_WIDTH)),
                  pl.BlockSpec((r, A_WIDTH), lambda i: (i, OFF_AV // A_WIDTH)),
                  pl.BlockSpec((1, A_WIDTH), lambda i: (0, 0)),
                  pl.BlockSpec((A_GROUPS, A_BLOCK, A_BLOCK), lambda i: (0, 0, 0)),
                  pl.BlockSpec((A_GROUPS, A_BLOCK, A_GROUP), lambda i: (0, 0, 0))],
        out_specs=out_specs,
        out_shape=out_shape,
        compiler_params=_cparams(("parallel",), 32),
    )(proj2, proj2, nw.reshape(1, A_WIDTH), ws, bs_b)
    return res if emit_v else (res[0], None)


def _ret_kernel(q_ref, k_ref, v_ref, g_ref, cos_ref, sin_ref, dm_ref, qd_ref, kd_ref, sd_ref, nw_ref, *rest,
                has_init):
    if has_init:
        s0_ref, y_ref, sfin_ref, s_scr = rest
    else:
        y_ref, sfin_ref, s_scr = rest
    c = pl.program_id(1)

    @pl.when(c == 0)
    def _():
        s_scr[...] = s0_ref[...] if has_init else jnp.zeros_like(s_scr)

    cos = cos_ref[...]
    sin = sin_ref[...]
    nw = nw_ref[...]
    for h in range(B_HEADS):
        ks = slice(h * B_DK, (h + 1) * B_DK)
        vs = slice(h * B_DV, (h + 1) * B_DV)
        q = q_ref[:, ks].astype(F32)
        k = k_ref[:, ks].astype(F32)
        qr = q * cos + pltpu.roll(q, B_DK // 2, 1) * sin
        kr = (k * cos + pltpu.roll(k, B_DK // 2, 1) * sin) * (B_DK ** -0.5)
        qb = qr.astype(BF16)
        kb = kr.astype(BF16)
        v = v_ref[:, vs]
        attn = lax.dot_general(qb, kb, (((1,), (1,)), ((), ())), preferred_element_type=F32) * dm_ref[h]
        intra = jnp.dot(attn.astype(BF16), v, preferred_element_type=F32)
        s_prev = s_scr[h]
        inter = jnp.dot(qb, s_prev.astype(BF16), preferred_element_type=F32) * qd_ref[h]
        o = intra + inter
        kt = jnp.transpose(kr * kd_ref[h]).astype(BF16)
        s_scr[h] = sd_ref[h] * s_prev + jnp.dot(kt, v, preferred_element_type=F32)
        gate = g_ref[:, vs].astype(F32)
        y_ref[:, vs] = (_rms(o, nw) * (gate * jax.nn.sigmoid(gate))).astype(y_ref.dtype)

    @pl.when(c == pl.num_programs(1) - 1)
    def _():
        sfin_ref[...] = s_scr[...]


def _ret_tables(c):
    lg = jnp.log1p(-jnp.exp2(-5.0 - jnp.arange(B_HEADS, dtype=F32)))
    idx = jnp.arange(c, dtype=F32)
    diff = idx[:, None] - idx[None, :]
    dm = jnp.where(diff >= 0, jnp.exp(jnp.maximum(diff, 0.0)[None] * lg[:, None, None]), 0.0)
    qd = jnp.exp((idx + 1.0)[None, :] * lg[:, None])
    kd = jnp.exp((c - 1.0 - idx)[None, :] * lg[:, None])
    sd = jnp.exp(c * lg)
    return (dm, jnp.broadcast_to(qd[:, :, None], (B_HEADS, c, B_DV)),
            jnp.broadcast_to(kd[:, :, None], (B_HEADS, c, B_DK)),
            jnp.broadcast_to(sd[:, None, None], (B_HEADS, 1, B_DV)))


def _rope_tables(pos):
    half = B_DK // 2
    inv_freq = 1.0 / (ROPE_BASE ** jnp.linspace(0.0, 1.0, half, dtype=F32))
    ang = pos.astype(F32)[:, None] * inv_freq[None, :]
    cos, sin = jnp.cos(ang), jnp.sin(ang)
    return jnp.concatenate([cos, cos], axis=-1), jnp.concatenate([-sin, sin], axis=-1)


def _retention(proj3, pos, nw, s0):
    bsz, seq, _ = proj3.shape
    c = _pick(seq, 256)
    dm, qd, kd, sd = _ret_tables(c)
    cos, sin = _rope_tables(pos)
    has_init = s0 is not None
    qk_w = B_HEADS * B_DK
    v_w = B_HEADS * B_DV
    full = lambda shape: pl.BlockSpec(shape, lambda b, i: (0,) * len(shape))
    state = pl.BlockSpec((None, B_HEADS, B_DK, B_DV), lambda b, i: (b, 0, 0, 0))
    in_specs = [pl.BlockSpec((None, c, qk_w), lambda b, i: (b, i, OFF_BQ // qk_w)),
                pl.BlockSpec((None, c, qk_w), lambda b, i: (b, i, OFF_BK // qk_w)),
                pl.BlockSpec((None, c, v_w), lambda b, i: (b, i, OFF_BV // v_w)),
                pl.BlockSpec((None, c, v_w), lambda b, i: (b, i, OFF_BG // v_w)),
                pl.BlockSpec((c, B_DK), lambda b, i: (i, 0)),
                pl.BlockSpec((c, B_DK), lambda b, i: (i, 0)),
                full((B_HEADS, c, c)), full((B_HEADS, c, B_DV)), full((B_HEADS, c, B_DK)),
                full((B_HEADS, 1, B_DV)), full((1, B_DV))]
    args = [proj3, proj3, proj3, proj3, cos, sin, dm, qd, kd, sd, nw.reshape(1, B_DV)]
    if has_init:
        in_specs.append(state)
        args.append(s0)
    return pl.pallas_call(
        functools.partial(_ret_kernel, has_init=has_init),
        grid=(bsz, seq // c),
        in_specs=in_specs,
        out_specs=[pl.BlockSpec((None, c, v_w), lambda b, i: (b, i, 0)), state],
        out_shape=[jax.ShapeDtypeStruct((bsz, seq, v_w), BF16),
                   jax.ShapeDtypeStruct((bsz, B_HEADS, B_DK, B_DV), F32)],
        scratch_shapes=[pltpu.VMEM((B_HEADS, B_DK, B_DV), F32)],
        compiler_params=_cparams(("parallel", "arbitrary"), 32),
    )(*args)


def _group_mean_matrix():
    r = lax.broadcasted_iota(jnp.int32, (LANES, LANES), 0)
    c = lax.broadcasted_iota(jnp.int32, (LANES, LANES), 1)
    return jnp.where((r // C_DH) == (c // C_DH), 1.0 / C_DH, 0.0).astype(BF16)


def _cprep_kernel(q_ref, k_ref, v_ref, qw_ref, kw_ref, lam_ref, *rest, lam_init, emit_vt, n_prev, l):
    qn_ref, kn_ref, kf_ref, vf_ref, lamo_ref = rest[n_prev:n_prev + 5]
    rows = q_ref.shape[0]
    if n_prev == 0:
        for d in range(kf_ref.shape[0]):
            if d != l:
                kf_ref[d] = jnp.zeros(kf_ref.shape[1:], F32)
                vf_ref[d] = jnp.zeros(vf_ref.shape[1:], F32)
        kf_ref, vf_ref = kf_ref.at[l], vf_ref.at[l]
    gm = _group_mean_matrix()
    qw = qw_ref[...]
    kw = kw_ref[...]
    for h in range(C_HEADS):
        cs = slice(h * LANES, (h + 1) * LANES)
        hs = pl.ds(h, rows, stride=C_HEADS)
        x = q_ref[:, cs].astype(F32)
        ms = jnp.dot((x * x).astype(BF16), gm, preferred_element_type=F32)
        qn_ref[:, cs] = (x * lax.rsqrt(ms + EPS) * qw * Q_SCALE).astype(BF16)
        x = k_ref[:, cs].astype(F32)
        ms = jnp.dot((x * x).astype(BF16), gm, preferred_element_type=F32)
        kn = x * lax.rsqrt(ms + EPS) * kw
        kf_ref[hs, :] = kn
        kn_ref[:, cs] = kn.astype(BF16)
        vv = v_ref[:, cs].astype(F32)
        vf_ref[hs, :] = vv
        if emit_vt:
            vt_ref = rest[n_prev + 5]
            vt_ref[h, :C_DV, :] = jnp.transpose(vv).astype(BF16)
            vt_ref[h, C_DV:, :] = jnp.ones((VT_ROWS - C_DV, rows), BF16)
    lq = lam_ref[...]
    l01 = jnp.sum(lq[0:1] * lq[1:2], axis=-1, keepdims=True)
    l23 = jnp.sum(lq[2:3] * lq[3:4], axis=-1, keepdims=True)
    lam = jnp.exp(l01) - jnp.exp(l23) + lam_init
    lamo_ref[...] = jnp.broadcast_to(lam, lamo_ref.shape)


def _cprep(proj3, qw, kw, lam_p, lam_init, emit_vt, l, depth, prev_kv):
    bsz, seq, _ = proj3.shape
    r = _pick(seq, 512)
    w = C_HEADS * LANES
    qw2 = jnp.concatenate([qw, qw]).reshape(1, LANES)
    kw2 = jnp.concatenate([kw, kw]).reshape(1, LANES)
    blk = lambda off: pl.BlockSpec((None, r, w), lambda b, i: (b, i, off // w))
    row = pl.BlockSpec((None, r, w), lambda b, i: (b, i, 0))
    if prev_kv is None:
        kv = pl.BlockSpec((depth, None, r * C_HEADS, LANES), lambda b, i: (0, b, i, 0))
    else:
        kv = pl.BlockSpec((None, None, r * C_HEADS, LANES), lambda b, i: (l, b, i, 0))
    kv_shape = jax.ShapeDtypeStruct((depth, bsz, seq * C_HEADS, LANES), F32)
    out_shape = [jax.ShapeDtypeStruct((bsz, seq, w), BF16), jax.ShapeDtypeStruct((bsz, seq, w), BF16),
                 kv_shape, kv_shape, jax.ShapeDtypeStruct((8, LANES), F32)]
    out_specs = [row, row, kv, kv, pl.BlockSpec((8, LANES), lambda b, i: (0, 0))]
    if emit_vt:
        out_shape.append(jax.ShapeDtypeStruct((bsz, C_HEADS, VT_ROWS, seq), BF16))
        out_specs.append(pl.BlockSpec((None, C_HEADS, VT_ROWS, r), lambda b, i: (b, 0, 0, i)))
    in_specs = [blk(OFF_CQ), blk(OFF_CK), blk(OFF_CV),
                pl.BlockSpec((1, LANES), lambda b, i: (0, 0)),
                pl.BlockSpec((1, LANES), lambda b, i: (0, 0)),
                pl.BlockSpec((4, C_DH), lambda b, i: (0, 0))]
    args = [proj3, proj3, proj3, qw2, kw2, lam_p]
    aliases = {}
    if prev_kv is not None:
        in_specs += [pl.BlockSpec(memory_space=pl.ANY)] * 2
        aliases = {len(args): 2, len(args) + 1: 3}
        args += list(prev_kv)
    return pl.pallas_call(
        functools.partial(_cprep_kernel, lam_init=lam_init, emit_vt=emit_vt, n_prev=len(aliases), l=l),
        grid=(bsz, seq // r),
        in_specs=in_specs,
        out_specs=out_specs,
        out_shape=out_shape,
        input_output_aliases=aliases,
        compiler_params=_cparams(("arbitrary", "arbitrary"), 40),
    )(*args)


def _dattn_kernel(q_ref, k_ref, vt_ref, lam_ref, sw_ref, o_ref, s_a, s_b, m_scr, a_scr, *, t, out_scale):
    qi = pl.program_id(2)
    lane = lax.broadcasted_iota(jnp.int32, (t, LANES), 1)
    qc = []
    for hh in range(HEADS_PER_STEP):
        q = q_ref[:, hh * LANES:(hh + 1) * LANES]
        qc += [jnp.where(lane < C_DH, q, jnp.zeros_like(q)), jnp.where(lane >= C_DH, q, jnp.zeros_like(q))]
    m_scr[...] = jnp.full(m_scr.shape, NEG_BIG, F32)
    a_scr[...] = jnp.zeros(a_scr.shape, F32)

    def scores(blk, s_ref):
        rows = pl.ds(pl.multiple_of(blk * t, t), t)
        for ch in range(2 * HEADS_PER_STEP):
            hh = ch // 2
            kb = k_ref[rows, hh * LANES:(hh + 1) * LANES]
            s_ref[ch] = lax.dot_general(kb, qc[ch], (((1,), (1,)), ((), ())), preferred_element_type=F32)

    def update(blk, s_ref, masked=False):
        cols = pl.ds(pl.multiple_of(blk * t, t), t)
        if masked:
            kpos = lax.broadcasted_iota(jnp.int32, (t, t), 0)
            qpos = lax.broadcasted_iota(jnp.int32, (t, t), 1)
            vis = (kpos // CHUNK) <= (qpos // CHUNK)
        for ch in range(2 * HEADS_PER_STEP):
            s = s_ref[ch]
            if masked:
                s = jnp.where(vis, s, NEG_BIG)
            m_old = m_scr[ch]
            m_new = jnp.maximum(m_old, jnp.max(s, axis=0, keepdims=True))
            alpha = jnp.exp2(m_old - m_new)
            p = jnp.exp2(s - m_new).astype(BF16)
            a_scr[ch] = alpha * a_scr[ch] + jnp.dot(vt_ref[ch // 2, :, cols], p, preferred_element_type=F32)
            m_scr[ch] = m_new

    scores(qi, s_a)

    @pl.when(qi == 0)
    def _():
        update(qi, s_a, masked=True)

    @pl.when(qi > 0)
    def _():
        scores(0, s_b)
        update(qi, s_a, masked=True)

        def pair(jj, carry):
            j = 2 * jj
            scores(j + 1, s_a)
            update(j, s_b)
            scores(j + 2, s_b)
            update(j + 1, s_a)
            return carry

        lax.fori_loop(0, (qi - 1) // 2, pair, 0)

        @pl.when(qi % 2 == 1)
        def _():
            update(qi - 1, s_b)

        @pl.when(qi % 2 == 0)
        def _():
            scores(qi - 1, s_a)
            update(qi - 2, s_b)
            update(qi - 1, s_a)

    lam = lam_ref[0:1, 0:1]
    for hh in range(HEADS_PER_STEP):
        c0, c1 = 2 * hh, 2 * hh + 1
        o_t = (a_scr[c0, :C_DV, :] / a_scr[c0, C_DV:C_DV + 1, :]
               - lam * (a_scr[c1, :C_DV, :] / a_scr[c1, C_DV:C_DV + 1, :]))
        ms = jnp.mean(o_t * o_t, axis=0, keepdims=True)
        o = jnp.transpose(o_t * lax.rsqrt(ms + EPS))
        o_ref[:, hh * C_DV:(hh + 1) * C_DV] = (o * sw_ref[...] * out_scale).astype(o_ref.dtype)


def _dattn_prompt(qn, kn, vt, lam_b, sw, out_scale):
    bsz, seq, w = qn.shape
    t = _pick(seq, 512)
    hw = HEADS_PER_STEP * LANES
    chains = 2 * HEADS_PER_STEP
    return pl.pallas_call(
        functools.partial(_dattn_kernel, t=t, out_scale=out_scale),
        grid=(bsz, C_HEADS // HEADS_PER_STEP, seq // t),
        in_specs=[pl.BlockSpec((None, t, hw), lambda b, h, i: (b, i, h)),
                  pl.BlockSpec((None, seq, hw), lambda b, h, i: (b, 0, h), pipeline_mode=pl.Buffered(1)),
                  pl.BlockSpec((None, HEADS_PER_STEP, VT_ROWS, seq), lambda b, h, i: (b, h, 0, 0),
                               pipeline_mode=pl.Buffered(1)),
                  pl.BlockSpec((8, LANES), lambda b, h, i: (0, 0)),
                  pl.BlockSpec((1, C_DV), lambda b, h, i: (0, 0))],
        out_specs=pl.BlockSpec((None, t, hw), lambda b, h, i: (b, i, h)),
        out_shape=jax.ShapeDtypeStruct((bsz, seq, w), BF16),
        scratch_shapes=[pltpu.VMEM((chains, t, t), F32), pltpu.VMEM((chains, t, t), F32),
                        pltpu.VMEM((chains, 1, t), F32), pltpu.VMEM((chains, VT_ROWS, t), F32)],
        compiler_params=_cparams(("parallel", "parallel", "arbitrary"), 52),
    )(qn, kn, vt, lam_b, sw.reshape(1, C_DV))


def _dattn_sample_kernel(q_ref, kc_ref, vc_ref, kn_ref, vn_ref, lam_ref, sw_ref, o_ref, *, past, out_scale):
    lq = q_ref.shape[0]
    lane = lax.broadcasted_iota(jnp.int32, (lq, LANES), 1)
    qpos_c = past + lax.broadcasted_iota(jnp.int32, (lq, past), 0)
    kpos_c = lax.broadcasted_iota(jnp.int32, (lq, past), 1)
    vis_c = (kpos_c // CHUNK) <= (qpos_c // CHUNK)
    qpos_n = past + lax.broadcasted_iota(jnp.int32, (lq, lq), 0)
    kpos_n = past + lax.broadcasted_iota(jnp.int32, (lq, lq), 1)
    vis_n = (kpos_n // CHUNK) <= (qpos_n // CHUNK)
    lam = lam_ref[0:1, 0:1]
    nt = (((1,), (1,)), ((), ()))
    for h in range(C_HEADS):
        cs = slice(h * LANES, (h + 1) * LANES)
        q = q_ref[:, cs]
        kc = kc_ref[pl.ds(h, past, stride=C_HEADS), :].astype(BF16)
        vc = vc_ref[pl.ds(h, past, stride=C_HEADS), :].astype(BF16)
        kn = kn_ref[:, cs]
        vn = vn_ref[:, cs]
        outs = []
        for c in range(2):
            qc = jnp.where((lane // C_DH) == c, q, jnp.zeros_like(q))
            s_c = jnp.where(vis_c, lax.dot_general(qc, kc, nt, preferred_element_type=F32), NEG_BIG)
            s_n = jnp.where(vis_n, lax.dot_general(qc, kn, nt, preferred_element_type=F32), NEG_BIG)
            m = jnp.maximum(jnp.max(s_c, axis=-1, keepdims=True), jnp.max(s_n, axis=-1, keepdims=True))
            p_c = jnp.exp2(s_c - m)
            p_n = jnp.exp2(s_n - m)
            l = jnp.sum(p_c, axis=-1, keepdims=True) + jnp.sum(p_n, axis=-1, keepdims=True)
            acc = (jnp.dot(p_c.astype(BF16), vc, preferred_element_type=F32)
                   + jnp.dot(p_n.astype(BF16), vn, preferred_element_type=F32))
            outs.append(acc / l)
        o = outs[0] - lam * outs[1]
        o_ref[:, cs] = (_rms(o, sw_ref[...]) * out_scale).astype(o_ref.dtype)


def _dattn_sample(qn, kn, proj3, cache_k, cache_v, l, lam_b, sw, out_scale):
    bsz, lq, w = qn.shape
    depth, _, past = cache_k.shape[:3]
    ck = cache_k.reshape(depth, bsz, past * C_HEADS, LANES)
    cv = cache_v.reshape(depth, bsz, past * C_HEADS, LANES)
    rows = lambda r: pl.BlockSpec((None, r, w), lambda b: (b, 0, 0))
    cache = pl.BlockSpec((None, None, past * C_HEADS, LANES), lambda b: (l, b, 0, 0))
    return pl.pallas_call(
        functools.partial(_dattn_sample_kernel, past=past, out_scale=out_scale),
        grid=(bsz,),
        in_specs=[rows(lq), cache, cache, rows(lq),
                  pl.BlockSpec((None, lq, w), lambda b: (b, 0, OFF_CV // w)),
                  pl.BlockSpec((8, LANES), lambda b: (0, 0)),
                  pl.BlockSpec((1, C_DV), lambda b: (0, 0))],
        out_specs=rows(lq),
        out_shape=jax.ShapeDtypeStruct((bsz, lq, w), BF16),
        compiler_params=_cparams(("parallel",), 40),
    )(qn, ck, cv, kn, proj3, lam_b, sw.reshape(1, C_DV))


def _merge_kernel(ya_ref, yb_ref, yc_ref, ga_ref, gb_ref, gc_ref, w_ref, o_ref):
    acc = None
    for y_ref, g_ref, n in ((ya_ref, ga_ref, 0), (yb_ref, gb_ref, 1), (yc_ref, gc_ref, 2)):
        up = jnp.dot(y_ref[...], w_ref[n], preferred_element_type=F32)
        term = jax.nn.sigmoid(g_ref[...].astype(F32)) * up
        acc = term if acc is None else acc + term
    o_ref[...] = acc.astype(o_ref.dtype)


def _merge(ya, yb, yc, proj2, wb_bf, l):
    n = ya.shape[0]
    tm = _pick(n, 512)
    tn = 1024
    ybs = pl.BlockSpec((tm, A_WIDTH), lambda j, i: (i, 0))
    gate = lambda b: pl.BlockSpec((tm, tn), lambda j, i: (i, (OFF_GATE + b * D_MODEL) // tn + j))
    return pl.pallas_call(
        _merge_kernel,
        grid=(D_MODEL // tn, n // tm),
        in_specs=[ybs, ybs, ybs, gate(0), gate(1), gate(2),
                  pl.BlockSpec((None, 3, A_WIDTH, tn), lambda j, i: (l, 0, 0, j))],
        out_specs=pl.BlockSpec((tm, tn), lambda j, i: (i, j)),
        out_shape=jax.ShapeDtypeStruct((n, D_MODEL), BF16),
        compiler_params=_cparams(("parallel", "parallel"), 48),
    )(ya, yb, yc, proj2, proj2, proj2, wb_bf)


ROUTE_GROUP_LANE = N_EXPERTS


def _outproj_router_kernel(u_ref, w_ref, x_ref, nw_ref, wcat_ref, br_ref, xo_ref, o_ref, ot_ref, cnt_ref, run_scr):
    @pl.when(pl.program_id(0) == 0)
    def _():
        run_scr[...] = jnp.zeros_like(run_scr)

    tm = x_ref.shape[0]
    x_new = x_ref[...] + jnp.dot(u_ref[...], w_ref[...], preferred_element_type=F32)
    xo_ref[...] = x_new
    xn = _rms(x_new, nw_ref[...])
    hi = xn.astype(BF16)
    lo = (xn - hi.astype(F32)).astype(BF16)
    both = jnp.dot(hi, wcat_ref[...], preferred_element_type=F32)
    logit = (both[:, :LANES] + both[:, LANES:]
             + jnp.dot(lo, wcat_ref[:, :LANES], preferred_element_type=F32) + br_ref[...])
    lane = lax.broadcasted_iota(jnp.int32, logit.shape, 1).astype(F32)
    big = 1e9
    is_g = (lane >= ROUTE_GROUP_LANE) & (lane < ROUTE_GROUP_LANE + N_GROUPS)
    lg = jnp.where(is_g, logit, NEG_BIG)
    mg = jnp.max(lg, axis=-1, keepdims=True)
    g_sel = jnp.min(jnp.where(lg == mg, lane - ROUTE_GROUP_LANE, big), axis=-1, keepdims=True)
    p_group = 1.0 / jnp.sum(jnp.exp(lg - mg), axis=-1, keepdims=True)
    lo_lane = g_sel * EXPERTS_PER_GROUP
    in_g = (lane >= lo_lane) & (lane < lo_lane + EXPERTS_PER_GROUP)
    le = jnp.where(in_g, logit, NEG_BIG)
    v1 = jnp.max(le, axis=-1, keepdims=True)
    i1 = jnp.min(jnp.where(le == v1, lane, big), axis=-1, keepdims=True)
    le2 = jnp.where(lane == i1, NEG_BIG, le)
    v2 = jnp.max(le2, axis=-1, keepdims=True)
    i2 = jnp.min(jnp.where(le2 == v2, lane, big), axis=-1, keepdims=True)
    e2 = jnp.exp(v2 - v1)
    den = 1.0 + e2
    p1 = p_group / den
    p2 = p_group * e2 / den
    oh1 = jnp.where(lane == i1, 1.0, 0.0)
    oh2 = jnp.where(lane == i2, 1.0, 0.0)
    oh = oh1 + oh2
    row = lax.broadcasted_iota(jnp.int32, (tm, tm), 0)
    col = lax.broadcasted_iota(jnp.int32, (tm, tm), 1)
    tri = jnp.where(row > col, 1.0, 0.0).astype(BF16)
    before = run_scr[...] + jnp.dot(tri, oh.astype(BF16), preferred_element_type=F32)
    r1 = jnp.sum(oh1 * before, axis=-1, keepdims=True)
    r2 = jnp.sum(oh2 * before, axis=-1, keepdims=True)
    run = run_scr[...] + jnp.sum(oh, axis=0, keepdims=True)
    run_scr[...] = run
    cnt_ref[...] = jnp.broadcast_to(run, cnt_ref.shape)
    out = jnp.zeros_like(logit)
    for n, val in enumerate((i1, i2, p1, p2, r1, r2)):
        out = jnp.where(lane == n, val, out)
    o_ref[...] = out
    ot_ref[...] = jnp.transpose(out)[:8]


def _outproj_router(u, w_bf, x2, l, nw, w_rg, b_rg, w_re, b_re):
    n = x2.shape[0]
    tm = _pick(n, 512)
    w = jnp.zeros((D_MODEL, LANES), F32).at[:, :N_EXPERTS].set(w_re).at[:, N_EXPERTS:N_EXPERTS + N_GROUPS].set(w_rg)
    br = jnp.zeros((1, LANES), F32).at[0, :N_EXPERTS].set(b_re).at[0, N_EXPERTS:N_EXPERTS + N_GROUPS].set(b_rg)
    whi = w.astype(BF16)
    wlo = (w - whi.astype(F32)).astype(BF16)
    full = lambda shape: pl.BlockSpec(shape, lambda i: (0, 0))
    rows = pl.BlockSpec((tm, D_MODEL), lambda i: (i, 0))
    return pl.pallas_call(
        _outproj_router_kernel,
        grid=(n // tm,),
        in_specs=[rows, pl.BlockSpec((None, D_MODEL, D_MODEL), lambda i: (l, 0, 0)), rows,
                  full((1, D_MODEL)), full((D_MODEL, 2 * LANES)), full((1, LANES))],
        out_specs=[rows, pl.BlockSpec((tm, LANES), lambda i: (i, 0)), pl.BlockSpec((8, tm), lambda i: (0, i)),
                   full((8, LANES))],
        out_shape=[jax.ShapeDtypeStruct((n, D_MODEL), F32), jax.ShapeDtypeStruct((n, LANES), F32),
                   jax.ShapeDtypeStruct((8, n), F32), jax.ShapeDtypeStruct((8, LANES), F32)],
        scratch_shapes=[pltpu.VMEM((1, LANES), F32)],
        compiler_params=_cparams(("arbitrary",), 48),
    )(u, w_bf, x2, nw.reshape(1, D_MODEL), jnp.concatenate([whi, wlo], axis=1), br)


def _moe_plan(e1, e2, r1, r2, cnt, tm_e):
    n = e1.shape[0]
    ar = jnp.arange(N_EXPERTS, dtype=jnp.int32)
    offs = jnp.cumsum(cnt) - cnt
    dest = jnp.stack([jnp.take(offs, e1) + r1, jnp.take(offs, e2) + r2])
    n_tiles = (2 * n) // tm_e
    first = offs // tm_e
    last = jnp.where(cnt > 0, (offs + cnt - 1) // tm_e, first - 1)
    n_items = last - first + 1
    item_end = jnp.cumsum(n_items)
    item_start = item_end - n_items
    n_work = n_tiles + N_EXPERTS - 1
    w = jnp.arange(n_work, dtype=jnp.int32)
    e_w = jnp.minimum(jnp.sum((w[:, None] >= item_end[None, :]).astype(jnp.int32), axis=1), N_EXPERTS - 1)
    valid = w < item_end[-1]
    ohw = (e_w[:, None] == ar[None, :]).astype(jnp.int32)
    sel = lambda tab: jnp.sum(ohw * tab[None, :], axis=1)
    tile = sel(first) + (w - sel(item_start))
    lo = jnp.maximum(sel(offs), tile * tm_e)
    hi = jnp.minimum(sel(offs) + sel(cnt), (tile + 1) * tm_e)
    last_e = jnp.max(jnp.where(n_items > 0, ar, 0))
    tile = jnp.where(valid, tile, n_tiles - 1)
    e_w = jnp.where(valid, e_w, last_e)
    lo = jnp.where(valid, lo, 0)
    hi = jnp.where(valid, hi, 0)
    return dest.astype(jnp.int32), tile.astype(jnp.int32), e_w.astype(jnp.int32), lo.astype(jnp.int32), hi.astype(jnp.int32)


def _row_copy(src_ref, src_row, dst_ref, dst_row, sem):
    return pltpu.make_async_copy(src_ref.at[pl.ds(src_row, 1), :], dst_ref.at[pl.ds(dst_row, 1), :], sem)


ROW_UNROLL = 8


def _dispatch_kernel(dest_ref, x_ref, xs_ref, sem, *, tm):
    def copies(i):
        for u in range(ROW_UNROLL):
            r = i * ROW_UNROLL + u
            for k in range(2):
                yield _row_copy(x_ref, r, xs_ref, dest_ref[k, r], sem)

    def issue(i, carry):
        for cp in copies(i):
            cp.start()
        return carry

    def drain(i, carry):
        for cp in copies(i):
            cp.wait()
        return carry

    lax.fori_loop(0, tm // ROW_UNROLL, issue, 0)
    lax.fori_loop(0, tm // ROW_UNROLL, drain, 0)


def _dispatch(x2, dest3, tm):
    n = x2.shape[0]
    return pl.pallas_call(
        functools.partial(_dispatch_kernel, tm=tm),
        grid=(n // tm,),
        in_specs=[pl.BlockSpec((None, 2, tm), lambda i: (i, 0, 0), memory_space=pltpu.SMEM),
                  pl.BlockSpec((tm, D_MODEL), lambda i: (i, 0))],
        out_specs=pl.BlockSpec(memory_space=pl.ANY),
        out_shape=jax.ShapeDtypeStruct((2 * n, D_MODEL), F32),
        scratch_shapes=[pltpu.SemaphoreType.DMA(())],
        compiler_params=_cparams(("arbitrary",), 32),
    )(dest3, x2)


def _experts_kernel(tile_ref, exp_ref, lo_ref, hi_ref, xs_ref, nw_ref, wg_ref, wu_ref, wd_ref, ys_ref, *, tm):
    w = pl.program_id(0)
    lo = lo_ref[w]
    hi = hi_ref[w]
    tile = tile_ref[w]
    first_visit = jnp.logical_or(w == 0, tile_ref[jnp.maximum(w - 1, 0)] != tile)

    @pl.when(hi > lo)
    def _():
        xn = _rms(xs_ref[...], nw_ref[...]).astype(BF16)
        gate = jnp.dot(xn, wg_ref[...].astype(BF16), preferred_element_type=F32)
        up = jnp.dot(xn, wu_ref[...].astype(BF16), preferred_element_type=F32)
        h = (gate * jax.nn.sigmoid(gate) * up).astype(BF16)
        y = jnp.dot(h, wd_ref[...].astype(BF16), preferred_element_type=F32)

        @pl.when(first_visit)
        def _():
            ys_ref[...] = y

        @pl.when(jnp.logical_not(first_visit))
        def _():
            rows = tile * tm + lax.broadcasted_iota(jnp.int32, (tm, 1), 0)
            ys_ref[...] = jnp.where((rows >= lo) & (rows < hi), y, ys_ref[...])


def _experts(xs, nw, wg_bf, wu_bf, wd_bf, l, tile, exp, lo, hi, tm):
    n_work = tile.shape[0]
    grid_spec = pltpu.PrefetchScalarGridSpec(
        num_scalar_prefetch=4,
        grid=(n_work,),
        in_specs=[pl.BlockSpec((tm, D_MODEL), lambda w, t, e, lo, hi: (t[w], 0)),
                  pl.BlockSpec((1, D_MODEL), lambda w, t, e, lo, hi: (0, 0)),
                  pl.BlockSpec((None, None, D_MODEL, D_EXPERT), lambda w, t, e, lo, hi: (l, e[w], 0, 0)),
                  pl.BlockSpec((None, None, D_MODEL, D_EXPERT), lambda w, t, e, lo, hi: (l, e[w], 0, 0)),
                  pl.BlockSpec((None, None, D_EXPERT, D_MODEL), lambda w, t, e, lo, hi: (l, e[w], 0, 0))],
        out_specs=pl.BlockSpec((tm, D_MODEL), lambda w, t, e, lo, hi: (t[w], 0)),
    )
    return pl.pallas_call(
        functools.partial(_experts_kernel, tm=tm),
        grid_spec=grid_spec,
        out_shape=jax.ShapeDtypeStruct(xs.shape, F32),
        compiler_params=_cparams(("arbitrary",), 56),
    )(tile, exp, lo, hi, xs, nw.reshape(1, D_MODEL), wg_bf, wu_bf, wd_bf)


def _combine_kernel(dest_ref, route_ref, x_ref, ys_ref, o_ref, buf0, buf1, sems, *, tm, n_tok):
    i = pl.program_id(0)

    def copies(step, slot, j):
        for u in range(ROW_UNROLL):
            r = j * ROW_UNROLL + u
            tok = step * tm + r
            yield _row_copy(ys_ref, dest_ref[tok], buf0.at[slot], r, sems.at[slot])
            yield _row_copy(ys_ref, dest_ref[n_tok + tok], buf1.at[slot], r, sems.at[slot])

    def issue(step, slot):
        def body(j, carry):
            for cp in copies(step, slot, j):
                cp.start()
            return carry

        lax.fori_loop(0, tm // ROW_UNROLL, body, 0)

    def drain(step, slot):
        def body(j, carry):
            for cp in copies(step, slot, j):
                cp.wait()
            return carry

        lax.fori_loop(0, tm // ROW_UNROLL, body, 0)

    slot = i % 2

    @pl.when(i == 0)
    def _():
        issue(0, 0)

    @pl.when(i + 1 < pl.num_programs(0))
    def _():
        issue(i + 1, 1 - slot)

    drain(i, slot)
    route = route_ref[...]
    o_ref[...] = x_ref[...] + route[:, 2:3] * buf0[slot] + route[:, 3:4] * buf1[slot]


def _combine(x2, route, ys, dest, tm):
    n = x2.shape[0]
    grid_spec = pltpu.PrefetchScalarGridSpec(
        num_scalar_prefetch=1,
        grid=(n // tm,),
        in_specs=[pl.BlockSpec((tm, LANES), lambda i, d: (i, 0)),
                  pl.BlockSpec((tm, D_MODEL), lambda i, d: (i, 0)),
                  pl.BlockSpec(memory_space=pl.ANY)],
        out_specs=pl.BlockSpec((tm, D_MODEL), lambda i, d: (i, 0)),
        scratch_shapes=[pltpu.VMEM((2, tm, D_MODEL), F32), pltpu.VMEM((2, tm, D_MODEL), F32),
                        pltpu.SemaphoreType.DMA((2,))],
    )
    return pl.pallas_call(
        functools.partial(_combine_kernel, tm=tm, n_tok=n),
        grid_spec=grid_spec,
        out_shape=jax.ShapeDtypeStruct((n, D_MODEL), F32),
        compiler_params=_cparams(("arbitrary",), 48),
    )(dest.reshape(2 * n), route, x2, ys)


def _ffn(x2, route, route_t, cnt, lw, l):
    n = x2.shape[0]
    fields = route_t.astype(jnp.int32)
    tm_e = _pick(2 * n, 512) if n >= 4096 else 128
    tm_r = _pick(n, 512)
    dest, tile, exp, lo, hi = _moe_plan(fields[0], fields[1], fields[4], fields[5],
                                        cnt[0, :N_EXPERTS].astype(jnp.int32), tm_e)
    dest3 = dest.reshape(2, n // tm_r, tm_r).transpose(1, 0, 2)
    xs = _dispatch(x2, dest3, tm_r)
    ys = _experts(xs, lw["norm_ffn_w"], lw["wg"], lw["wu"], lw["wd"], l, tile, exp, lo, hi, tm_e)
    return _combine(x2, route, ys, dest, tm_r)


def _layer(x3, pos, l, depth, lw, ret_state, kv_cache, prev_kv):
    bsz, seq, _ = x3.shape
    n = bsz * seq
    sample = ret_state is not None
    lam_init = 0.8 - 0.6 * math.exp(-0.3 * l)
    x2 = x3.reshape(n, D_MODEL)
    proj2 = _inproj(x2, lw["norm_mix_w"], lw["w_in"], l)
    proj3 = proj2.reshape(bsz, seq, D_IN)

    a_y, a_v = _gmlp(proj2, seq, lw["a_norm_w"], lw["a_ws"], lw["a_bs_b"], emit_v=sample)
    b_y, s_new = _retention(proj3, pos, lw["b_norm_w"], ret_state)
    prep = _cprep(proj3, lw["c_qnorm_w"], lw["c_knorm_w"], lw["c_lambda"], lam_init, not sample, l, depth, prev_kv)
    qn, kn, kf_all, vf_all, lam_b = prep[:5]
    if sample:
        c_y = _dattn_sample(qn, kn, proj3, kv_cache[0], kv_cache[1], l, lam_b, lw["c_subln_w"], 1.0 - lam_init)
    else:
        c_y = _dattn_prompt(qn, kn, prep[5], lam_b, lw["c_subln_w"], 1.0 - lam_init)

    u = _merge(a_y, b_y.reshape(n, A_WIDTH), c_y.reshape(n, A_WIDTH), proj2, lw["w_branch"], l)
    x2, route, route_t, cnt = _outproj_router(u, lw["w_out"], x2, l, lw["norm_ffn_w"], lw["w_rg"], lw["b_rg"],
                                              lw["w_re"], lw["b_re"])
    x2 = _ffn(x2, route, route_t, cnt, lw, l)
    return x2.reshape(bsz, seq, D_MODEL), (kf_all, vf_all), s_new, a_v


def kernel(x_prompt, x_sample, cache_k_c, cache_v_c, state_ret, norm_mix_w, w_in, a_norm_w, a_ws, a_bs, b_norm_w, c_qnorm_w, c_knorm_w, c_lambda, c_subln_w, w_branch, w_out, norm_ffn_w, w_router_group, b_router_group, w_router_expert, b_router_expert, w_gate_e, w_up_e, w_down_e):
    depth = w_in.shape[0]
    past = cache_k_c.shape[2]
    pos_p = jnp.arange(x_prompt.shape[1])
    pos_s = past + jnp.arange(x_sample.shape[1])
    yp, ys = x_prompt, x_sample
    big = dict(w_in=w_in.astype(BF16), w_branch=w_branch.astype(BF16), w_out=w_out.astype(BF16),
               wg=w_gate_e, wu=w_up_e, wd=w_down_e)
    kv_p = kv_s = None
    rets_p, rets_s, avs = [], [], []
    for l in range(depth):
        lw = dict(
            big, norm_mix_w=norm_mix_w[l], a_norm_w=a_norm_w[l], a_ws=a_ws[l],
            a_bs_b=jnp.broadcast_to(a_bs[l][:, :, None], (A_GROUPS, A_BLOCK, A_GROUP)),
            b_norm_w=b_norm_w[l], c_qnorm_w=c_qnorm_w[l], c_knorm_w=c_knorm_w[l], c_lambda=c_lambda[l],
            c_subln_w=c_subln_w[l], norm_ffn_w=norm_ffn_w[l], w_rg=w_router_group[l], b_rg=b_router_group[l],
            w_re=w_router_expert[l], b_re=b_router_expert[l])
        yp, kv_p, rp, _ = _layer(yp, pos_p, l, depth, lw, None, None, kv_p)
        ys, kv_s, rn, avn = _layer(ys, pos_s, l, depth, lw, state_ret[l], (cache_k_c, cache_v_c), kv_s)
        rets_p.append(rp)
        rets_s.append(rn)
        avs.append(avn.reshape(ys.shape[0], ys.shape[1], A_WIDTH))
    as_cache = lambda buf, x: buf.reshape(depth, x.shape[0], x.shape[1], C_HEADS, LANES)
    return (yp, ys, as_cache(kv_p[0], x_prompt), as_cache(kv_p[1], x_prompt), jnp.stack(rets_p, 0),
            as_cache(kv_s[0], x_sample), as_cache(kv_s[1], x_sample), jnp.stack(rets_s, 0), jnp.stack(avs, 0))
```

```python
import functools
import math

import jax
import jax.numpy as jnp
from jax import lax
from jax.experimental import pallas as pl
from jax.experimental.pallas import tpu as pltpu

F32 = jnp.float32
BF16 = jnp.bfloat16

D_MODEL = 2048
CHUNK = 64
A_WIDTH = 1024
A_BLOCK = 128
A_GROUP = 128
A_GROUPS = 8
B_HEADS = 4
B_DK = 128
B_DV = 256
ROPE_BASE = 10000.0
C_HEADS = 8
C_DH = 64
C_DV = 128
N_GROUPS = 4
EXPERTS_PER_GROUP = 4
N_EXPERTS = 16
D_EXPERT = 512
EPS = 1e-6
D_IN = 14336

OFF_AU, OFF_AV, OFF_BQ, OFF_BK, OFF_BV, OFF_BG, OFF_CQ, OFF_CK, OFF_CV, OFF_GATE = (
    0, 1024, 2048, 2560, 3072, 4096, 5120, 6144, 7168, 8192)

LANES = 128
MIB = 1024 * 1024
NEG_BIG = -1e30
Q_SCALE = (C_DH ** -0.5) * math.log2(math.e)
VT_ROWS = C_DV + 16
HEADS_PER_STEP = 4


def _cparams(sem, vmem_mib):
    return pltpu.CompilerParams(dimension_semantics=sem, vmem_limit_bytes=vmem_mib * MIB)


def _rms(x, w):
    ms = jnp.mean(x * x, axis=-1, keepdims=True)
    return x * lax.rsqrt(ms + EPS) * w


def _pick(n, pref):
    t = min(pref, n)
    while n % t:
        t //= 2
    return t


def _inproj_kernel(x_ref, g_ref, w_ref, o_ref, xn_ref):
    @pl.when(pl.program_id(1) == 0)
    def _():
        xn_ref[...] = _rms(x_ref[...], g_ref[...]).astype(BF16)

    o_ref[...] = jnp.dot(xn_ref[...], w_ref[...], preferred_element_type=F32).astype(o_ref.dtype)


def _inproj(x2, g, w_bf, l):
    n = x2.shape[0]
    tm = _pick(n, 1024)
    tn = 1024
    return pl.pallas_call(
        _inproj_kernel,
        grid=(n // tm, D_IN // tn),
        in_specs=[pl.BlockSpec((tm, D_MODEL), lambda i, j: (i, 0)),
                  pl.BlockSpec((1, D_MODEL), lambda i, j: (0, 0)),
                  pl.BlockSpec((None, D_MODEL, tn), lambda i, j: (l, 0, j))],
        out_specs=pl.BlockSpec((tm, tn), lambda i, j: (i, j)),
        out_shape=jax.ShapeDtypeStruct((n, D_IN), BF16),
        scratch_shapes=[pltpu.VMEM((tm, D_MODEL), BF16)],
        compiler_params=_cparams(("parallel", "arbitrary"), 48),
    )(x2, g.reshape(1, D_MODEL), w_bf)


def _gmlp_kernel(u_ref, v_ref, nw_ref, ws_ref, bs_ref, y_ref, *rest, pb, nblk, emit_v):
    u = jax.nn.gelu(u_ref[...].astype(F32))
    v = _rms(jax.nn.gelu(v_ref[...].astype(F32)), nw_ref[...])
    if emit_v:
        rest[0][...] = v
    vb = v.astype(BF16)
    p = lax.broadcasted_iota(jnp.int32, (A_BLOCK, A_BLOCK), 0)
    q = lax.broadcasted_iota(jnp.int32, (A_BLOCK, A_BLOCK), 1)
    mask = (q // CHUNK) <= (p // CHUNK)
    for g in range(A_GROUPS):
        w = jnp.where(mask, ws_ref[g], 0.0).astype(BF16)
        bias = bs_ref[g, :pb, :]
        cs = slice(g * A_GROUP, (g + 1) * A_GROUP)
        for b in range(nblk):
            rs = slice(b * pb, (b + 1) * pb)
            vg = vb[rs, cs]
            if pb < A_BLOCK:
                vg = jnp.concatenate([vg, jnp.zeros((A_BLOCK - pb, A_GROUP), BF16)], axis=0)
            s = jnp.dot(w, vg, preferred_element_type=F32)[:pb] + bias
            y_ref[rs, cs] = (u[rs, cs] * s).astype(y_ref.dtype)


def _gmlp(proj2, seq_len, nw, ws, bs_b, emit_v):
    n = proj2.shape[0]
    if seq_len % A_BLOCK == 0:
        pb, nblk = A_BLOCK, 4 if seq_len % (4 * A_BLOCK) == 0 else 1
    else:
        assert seq_len <= A_BLOCK
        pb, nblk = seq_len, 1
    r = pb * nblk
    out_shape = [jax.ShapeDtypeStruct((n, A_WIDTH), BF16)]
    out_specs = [pl.BlockSpec((r, A_WIDTH), lambda i: (i, 0))]
    if emit_v:
        out_shape.append(jax.ShapeDtypeStruct((n, A_WIDTH), F32))
        out_specs.append(pl.BlockSpec((r, A_WIDTH), lambda i: (i, 0)))
    res = pl.pallas_call(
        functools.partial(_gmlp_kernel, pb=pb, nblk=nblk, emit_v=emit_v),
        grid=(n // r,),
        in_specs=[pl.BlockSpec((r, A_WIDTH), lambda i: (i, OFF_AU // A_WIDTH)),
                  pl.BlockSpec((r, A_WIDTH), lambda i: (i, OFF_AV // A_WIDTH)),
                  pl.BlockSpec((1, A_WIDTH), lambda i: (0, 0)),
                  pl.BlockSpec((A_GROUPS, A_BLOCK, A_BLOCK), lambda i: (0, 0, 0)),
                  pl.BlockSpec((A_GROUPS, A_BLOCK, A_GROUP), lambda i: (0, 0, 0))],
        out_specs=out_specs,
        out_shape=out_shape,
        compiler_params=_cparams(("parallel",), 32),
    )(proj2, proj2, nw.reshape(1, A_WIDTH), ws, bs_b)
    return res if emit_v else (res[0], None)


def _ret_kernel(q_ref, k_ref, v_ref, g_ref, cos_ref, sin_ref, dm_ref, qd_ref, kd_ref, sd_ref, nw_ref, *rest,
                has_init):
    if has_init:
        s0_ref, y_ref, sfin_ref, s_scr = rest
    else:
        y_ref, sfin_ref, s_scr = rest
    c = pl.program_id(1)

    @pl.when(c == 0)
    def _():
        s_scr[...] = s0_ref[...] if has_init else jnp.zeros_like(s_scr)

    cos = cos_ref[...]
    sin = sin_ref[...]
    nw = nw_ref[...]
    for h in range(B_HEADS):
        ks = slice(h * B_DK, (h + 1) * B_DK)
        vs = slice(h * B_DV, (h + 1) * B_DV)
        q = q_ref[:, ks].astype(F32)
        k = k_ref[:, ks].astype(F32)
        qr = q * cos + pltpu.roll(q, B_DK // 2, 1) * sin
        kr = (k * cos + pltpu.roll(k, B_DK // 2, 1) * sin) * (B_DK ** -0.5)
        qb = qr.astype(BF16)
        kb = kr.astype(BF16)
        v = v_ref[:, vs]
        attn = lax.dot_general(qb, kb, (((1,), (1,)), ((), ())), preferred_element_type=F32) * dm_ref[h]
        intra = jnp.dot(attn.astype(BF16), v, preferred_element_type=F32)
        s_prev = s_scr[h]
        inter = jnp.dot(qb, s_prev.astype(BF16), preferred_element_type=F32) * qd_ref[h]
        o = intra + inter
        kt = jnp.transpose(kr * kd_ref[h]).astype(BF16)
        s_scr[h] = sd_ref[h] * s_prev + jnp.dot(kt, v, preferred_element_type=F32)
        gate = g_ref[:, vs].astype(F32)
        y_ref[:, vs] = (_rms(o, nw) * (gate * jax.nn.sigmoid(gate))).astype(y_ref.dtype)

    @pl.when(c == pl.num_programs(1) - 1)
    def _():
        sfin_ref[...] = s_scr[...]


def _ret_tables(c):
    lg = jnp.log1p(-jnp.exp2(-5.0 - jnp.arange(B_HEADS, dtype=F32)))
    idx = jnp.arange(c, dtype=F32)
    diff = idx[:, None] - idx[None, :]
    dm = jnp.where(diff >= 0, jnp.exp(jnp.maximum(diff, 0.0)[None] * lg[:, None, None]), 0.0)
    qd = jnp.exp((idx + 1.0)[None, :] * lg[:, None])
    kd = jnp.exp((c - 1.0 - idx)[None, :] * lg[:, None])
    sd = jnp.exp(c * lg)
    return (dm, jnp.broadcast_to(qd[:, :, None], (B_HEADS, c, B_DV)),
            jnp.broadcast_to(kd[:, :, None], (B_HEADS, c, B_DK)),
            jnp.broadcast_to(sd[:, None, None], (B_HEADS, 1, B_DV)))


def _rope_tables(pos):
    half = B_DK // 2
    inv_freq = 1.0 / (ROPE_BASE ** jnp.linspace(0.0, 1.0, half, dtype=F32))
    ang = pos.astype(F32)[:, None] * inv_freq[None, :]
    cos, sin = jnp.cos(ang), jnp.sin(ang)
    return jnp.concatenate([cos, cos], axis=-1), jnp.concatenate([-sin, sin], axis=-1)


def _retention(proj3, pos, nw, s0):
    bsz, seq, _ = proj3.shape
    c = _pick(seq, 256)
    dm, qd, kd, sd = _ret_tables(c)
    cos, sin = _rope_tables(pos)
    has_init = s0 is not None
    qk_w = B_HEADS * B_DK
    v_w = B_HEADS * B_DV
    full = lambda shape: pl.BlockSpec(shape, lambda b, i: (0,) * len(shape))
    state = pl.BlockSpec((None, B_HEADS, B_DK, B_DV), lambda b, i: (b, 0, 0, 0))
    in_specs = [pl.BlockSpec((None, c, qk_w), lambda b, i: (b, i, OFF_BQ // qk_w)),
                pl.BlockSpec((None, c, qk_w), lambda b, i: (b, i, OFF_BK // qk_w)),
                pl.BlockSpec((None, c, v_w), lambda b, i: (b, i, OFF_BV // v_w)),
                pl.BlockSpec((None, c, v_w), lambda b, i: (b, i, OFF_BG // v_w)),
                pl.BlockSpec((c, B_DK), lambda b, i: (i, 0)),
                pl.BlockSpec((c, B_DK), lambda b, i: (i, 0)),
                full((B_HEADS, c, c)), full((B_HEADS, c, B_DV)), full((B_HEADS, c, B_DK)),
                full((B_HEADS, 1, B_DV)), full((1, B_DV))]
    args = [proj3, proj3, proj3, proj3, cos, sin, dm, qd, kd, sd, nw.reshape(1, B_DV)]
    if has_init:
        in_specs.append(state)
        args.append(s0)
    return pl.pallas_call(
        functools.partial(_ret_kernel, has_init=has_init),
        grid=(bsz, seq // c),
        in_specs=in_specs,
        out_specs=[pl.BlockSpec((None, c, v_w), lambda b, i: (b, i, 0)), state],
        out_shape=[jax.ShapeDtypeStruct((bsz, seq, v_w), BF16),
                   jax.ShapeDtypeStruct((bsz, B_HEADS, B_DK, B_DV), F32)],
        scratch_shapes=[pltpu.VMEM((B_HEADS, B_DK, B_DV), F32)],
        compiler_params=_cparams(("parallel", "arbitrary"), 32),
    )(*args)


def _group_mean_matrix():
    r = lax.broadcasted_iota(jnp.int32, (LANES, LANES), 0)
    c = lax.broadcasted_iota(jnp.int32, (LANES, LANES), 1)
    return jnp.where((r // C_DH) == (c // C_DH), 1.0 / C_DH, 0.0).astype(BF16)


def _cprep_kernel(q_ref, k_ref, v_ref, qw_ref, kw_ref, lam_ref, *rest, lam_init, emit_vt, n_prev, l):
    qn_ref, kn_ref, kf_ref, vf_ref, lamo_ref = rest[n_prev:n_prev + 5]
    rows = q_ref.shape[0]
    if n_prev == 0:
        for d in range(kf_ref.shape[0]):
            if d != l:
                kf_ref[d] = jnp.zeros(kf_ref.shape[1:], F32)
                vf_ref[d] = jnp.zeros(vf_ref.shape[1:], F32)
        kf_ref, vf_ref = kf_ref.at[l], vf_ref.at[l]
    gm = _group_mean_matrix()
    qw = qw_ref[...]
    kw = kw_ref[...]
    for h in range(C_HEADS):
        cs = slice(h * LANES, (h + 1) * LANES)
        hs = pl.ds(h, rows, stride=C_HEADS)
        x = q_ref[:, cs].astype(F32)
        ms = jnp.dot((x * x).astype(BF16), gm, preferred_element_type=F32)
        qn_ref[:, cs] = (x * lax.rsqrt(ms + EPS) * qw * Q_SCALE).astype(BF16)
        x = k_ref[:, cs].astype(F32)
        ms = jnp.dot((x * x).astype(BF16), gm, preferred_element_type=F32)
        kn = x * lax.rsqrt(ms + EPS) * kw
        kf_ref[hs, :] = kn
        kn_ref[:, cs] = kn.astype(BF16)
        vv = v_ref[:, cs].astype(F32)
        vf_ref[hs, :] = vv
        if emit_vt:
            vt_ref = rest[n_prev + 5]
            vt_ref[h, :C_DV, :] = jnp.transpose(vv).astype(BF16)
            vt_ref[h, C_DV:, :] = jnp.ones((VT_ROWS - C_DV, rows), BF16)
    lq = lam_ref[...]
    l01 = jnp.sum(lq[0:1] * lq[1:2], axis=-1, keepdims=True)
    l23 = jnp.sum(lq[2:3] * lq[3:4], axis=-1, keepdims=True)
    lam = jnp.exp(l01) - jnp.exp(l23) + lam_init
    lamo_ref[...] = jnp.broadcast_to(lam, lamo_ref.shape)


def _cprep(proj3, qw, kw, lam_p, lam_init, emit_vt, l, depth, prev_kv):
    bsz, seq, _ = proj3.shape
    r = _pick(seq, 512)
    w = C_HEADS * LANES
    qw2 = jnp.concatenate([qw, qw]).reshape(1, LANES)
    kw2 = jnp.concatenate([kw, kw]).reshape(1, LANES)
    blk = lambda off: pl.BlockSpec((None, r, w), lambda b, i: (b, i, off // w))
    row = pl.BlockSpec((None, r, w), lambda b, i: (b, i, 0))
    if prev_kv is None:
        kv = pl.BlockSpec((depth, None, r * C_HEADS, LANES), lambda b, i: (0, b, i, 0))
    else:
        kv = pl.BlockSpec((None, None, r * C_HEADS, LANES), lambda b, i: (l, b, i, 0))
    kv_shape = jax.ShapeDtypeStruct((depth, bsz, seq * C_HEADS, LANES), F32)
    out_shape = [jax.ShapeDtypeStruct((bsz, seq, w), BF16), jax.ShapeDtypeStruct((bsz, seq, w), BF16),
                 kv_shape, kv_shape, jax.ShapeDtypeStruct((8, LANES), F32)]
    out_specs = [row, row, kv, kv, pl.BlockSpec((8, LANES), lambda b, i: (0, 0))]
    if emit_vt:
        out_shape.append(jax.ShapeDtypeStruct((bsz, C_HEADS, VT_ROWS, seq), BF16))
        out_specs.append(pl.BlockSpec((None, C_HEADS, VT_ROWS, r), lambda b, i: (b, 0, 0, i)))
    in_specs = [blk(OFF_CQ), blk(OFF_CK), blk(OFF_CV),
                pl.BlockSpec((1, LANES), lambda b, i: (0, 0)),
                pl.BlockSpec((1, LANES), lambda b, i: (0, 0)),
                pl.BlockSpec((4, C_DH), lambda b, i: (0, 0))]
    args = [proj3, proj3, proj3, qw2, kw2, lam_p]
    aliases = {}
    if prev_kv is not None:
        in_specs += [pl.BlockSpec(memory_space=pl.ANY)] * 2
        aliases = {len(args): 2, len(args) + 1: 3}
        args += list(prev_kv)
    return pl.pallas_call(
        functools.partial(_cprep_kernel, lam_init=lam_init, emit_vt=emit_vt, n_prev=len(aliases), l=l),
        grid=(bsz, seq // r),
        in_specs=in_specs,
        out_specs=out_specs,
        out_shape=out_shape,
        input_output_aliases=aliases,
        compiler_params=_cparams(("arbitrary", "arbitrary"), 40),
    )(*args)


def _dattn_kernel(q_ref, k_ref, vt_ref, lam_ref, sw_ref, o_ref, s_a, s_b, m_scr, a_scr, *, t, out_scale):
    qi = pl.program_id(2)
    lane = lax.broadcasted_iota(jnp.int32, (t, LANES), 1)
    qc = []
    for hh in range(HEADS_PER_STEP):
        q = q_ref[:, hh * LANES:(hh + 1) * LANES]
        qc += [jnp.where(lane < C_DH, q, jnp.zeros_like(q)), jnp.where(lane >= C_DH, q, jnp.zeros_like(q))]
    m_scr[...] = jnp.full(m_scr.shape, NEG_BIG, F32)
    a_scr[...] = jnp.zeros(a_scr.shape, F32)

    def scores(blk, s_ref):
        rows = pl.ds(pl.multiple_of(blk * t, t), t)
        for ch in range(2 * HEADS_PER_STEP):
            hh = ch // 2
            kb = k_ref[rows, hh * LANES:(hh + 1) * LANES]
            s_ref[ch] = lax.dot_general(kb, qc[ch], (((1,), (1,)), ((), ())), preferred_element_type=F32)

    def update(blk, s_ref, masked=False):
        cols = pl.ds(pl.multiple_of(blk * t, t), t)
        if masked:
            kpos = lax.broadcasted_iota(jnp.int32, (t, t), 0)
            qpos = lax.broadcasted_iota(jnp.int32, (t, t), 1)
            vis = (kpos // CHUNK) <= (qpos // CHUNK)
        for ch in range(2 * HEADS_PER_STEP):
            s = s_ref[ch]
            if masked:
                s = jnp.where(vis, s, NEG_BIG)
            m_old = m_scr[ch]
            m_new = jnp.maximum(m_old, jnp.max(s, axis=0, keepdims=True))
            alpha = jnp.exp2(m_old - m_new)
            p = jnp.exp2(s - m_new).astype(BF16)
            a_scr[ch] = alpha * a_scr[ch] + jnp.dot(vt_ref[ch // 2, :, cols], p, preferred_element_type=F32)
            m_scr[ch] = m_new

    scores(qi, s_a)

    @pl.when(qi == 0)
    def _():
        update(qi, s_a, masked=True)

    @pl.when(qi > 0)
    def _():
        scores(0, s_b)
        update(qi, s_a, masked=True)

        def pair(jj, carry):
            j = 2 * jj
            scores(j + 1, s_a)
            update(j, s_b)
            scores(j + 2, s_b)
            update(j + 1, s_a)
            return carry

        lax.fori_loop(0, (qi - 1) // 2, pair, 0)

        @pl.when(qi % 2 == 1)
        def _():
            update(qi - 1, s_b)

        @pl.when(qi % 2 == 0)
        def _():
            scores(qi - 1, s_a)
            update(qi - 2, s_b)
            update(qi - 1, s_a)

    lam = lam_ref[0:1, 0:1]
    for hh in range(HEADS_PER_STEP):
        c0, c1 = 2 * hh, 2 * hh + 1
        o_t = (a_scr[c0, :C_DV, :] / a_scr[c0, C_DV:C_DV + 1, :]
               - lam * (a_scr[c1, :C_DV, :] / a_scr[c1, C_DV:C_DV + 1, :]))
        ms = jnp.mean(o_t * o_t, axis=0, keepdims=True)
        o = jnp.transpose(o_t * lax.rsqrt(ms + EPS))
        o_ref[:, hh * C_DV:(hh + 1) * C_DV] = (o * sw_ref[...] * out_scale).astype(o_ref.dtype)


def _dattn_prompt(qn, kn, vt, lam_b, sw, out_scale):
    bsz, seq, w = qn.shape
    t = _pick(seq, 512)
    hw = HEADS_PER_STEP * LANES
    chains = 2 * HEADS_PER_STEP
    return pl.pallas_call(
        functools.partial(_dattn_kernel, t=t, out_scale=out_scale),
        grid=(bsz, C_HEADS // HEADS_PER_STEP, seq // t),
        in_specs=[pl.BlockSpec((None, t, hw), lambda b, h, i: (b, i, h)),
                  pl.BlockSpec((None, seq, hw), lambda b, h, i: (b, 0, h), pipeline_mode=pl.Buffered(1)),
                  pl.BlockSpec((None, HEADS_PER_STEP, VT_ROWS, seq), lambda b, h, i: (b, h, 0, 0),
                               pipeline_mode=pl.Buffered(1)),
                  pl.BlockSpec((8, LANES), lambda b, h, i: (0, 0)),
                  pl.BlockSpec((1, C_DV), lambda b, h, i: (0, 0))],
        out_specs=pl.BlockSpec((None, t, hw), lambda b, h, i: (b, i, h)),
        out_shape=jax.ShapeDtypeStruct((bsz, seq, w), BF16),
        scratch_shapes=[pltpu.VMEM((chains, t, t), F32), pltpu.VMEM((chains, t, t), F32),
                        pltpu.VMEM((chains, 1, t), F32), pltpu.VMEM((chains, VT_ROWS, t), F32)],
        compiler_params=_cparams(("parallel", "parallel", "arbitrary"), 52),
    )(qn, kn, vt, lam_b, sw.reshape(1, C_DV))


def _dattn_sample_kernel(q_ref, kc_ref, vc_ref, kn_ref, vn_ref, lam_ref, sw_ref, o_ref, *, past, out_scale):
    lq = q_ref.shape[0]
    lane = lax.broadcasted_iota(jnp.int32, (lq, LANES), 1)
    qpos_c = past + lax.broadcasted_iota(jnp.int32, (lq, past), 0)
    kpos_c = lax.broadcasted_iota(jnp.int32, (lq, past), 1)
    vis_c = (kpos_c // CHUNK) <= (qpos_c // CHUNK)
    qpos_n = past + lax.broadcasted_iota(jnp.int32, (lq, lq), 0)
    kpos_n = past + lax.broadcasted_iota(jnp.int32, (lq, lq), 1)
    vis_n = (kpos_n // CHUNK) <= (qpos_n // CHUNK)
    lam = lam_ref[0:1, 0:1]
    nt = (((1,), (1,)), ((), ()))
    for h in range(C_HEADS):
        cs = slice(h * LANES, (h + 1) * LANES)
        q = q_ref[:, cs]
        kc = kc_ref[pl.ds(h, past, stride=C_HEADS), :].astype(BF16)
        vc = vc_ref[pl.ds(h, past, stride=C_HEADS), :].astype(BF16)
        kn = kn_ref[:, cs]
        vn = vn_ref[:, cs]
        outs = []
        for c in range(2):
            qc = jnp.where((lane // C_DH) == c, q, jnp.zeros_like(q))
            s_c = jnp.where(vis_c, lax.dot_general(qc, kc, nt, preferred_element_type=F32), NEG_BIG)
            s_n = jnp.where(vis_n, lax.dot_general(qc, kn, nt, preferred_element_type=F32), NEG_BIG)
            m = jnp.maximum(jnp.max(s_c, axis=-1, keepdims=True), jnp.max(s_n, axis=-1, keepdims=True))
            p_c = jnp.exp2(s_c - m)
            p_n = jnp.exp2(s_n - m)
            l = jnp.sum(p_c, axis=-1, keepdims=True) + jnp.sum(p_n, axis=-1, keepdims=True)
            acc = (jnp.dot(p_c.astype(BF16), vc, preferred_element_type=F32)
                   + jnp.dot(p_n.astype(BF16), vn, preferred_element_type=F32))
            outs.append(acc / l)
        o = outs[0] - lam * outs[1]
        o_ref[:, cs] = (_rms(o, sw_ref[...]) * out_scale).astype(o_ref.dtype)


def _dattn_sample(qn, kn, proj3, cache_k, cache_v, l, lam_b, sw, out_scale):
    bsz, lq, w = qn.shape
    depth, _, past = cache_k.shape[:3]
    ck = cache_k.reshape(depth, bsz, past * C_HEADS, LANES)
    cv = cache_v.reshape(depth, bsz, past * C_HEADS, LANES)
    rows = lambda r: pl.BlockSpec((None, r, w), lambda b: (b, 0, 0))
    cache = pl.BlockSpec((None, None, past * C_HEADS, LANES), lambda b: (l, b, 0, 0))
    return pl.pallas_call(
        functools.partial(_dattn_sample_kernel, past=past, out_scale=out_scale),
        grid=(bsz,),
        in_specs=[rows(lq), cache, cache, rows(lq),
                  pl.BlockSpec((None, lq, w), lambda b: (b, 0, OFF_CV // w)),
                  pl.BlockSpec((8, LANES), lambda b: (0, 0)),
                  pl.BlockSpec((1, C_DV), lambda b: (0, 0))],
        out_specs=rows(lq),
        out_shape=jax.ShapeDtypeStruct((bsz, lq, w), BF16),
        compiler_params=_cparams(("parallel",), 40),
    )(qn, ck, cv, kn, proj3, lam_b, sw.reshape(1, C_DV))


def _merge_kernel(ya_ref, yb_ref, yc_ref, ga_ref, gb_ref, gc_ref, w_ref, o_ref):
    acc = None
    for y_ref, g_ref, n in ((ya_ref, ga_ref, 0), (yb_ref, gb_ref, 1), (yc_ref, gc_ref, 2)):
        up = jnp.dot(y_ref[...], w_ref[n], preferred_element_type=F32)
        term = jax.nn.sigmoid(g_ref[...].astype(F32)) * up
        acc = term if acc is None else acc + term
    o_ref[...] = acc.astype(o_ref.dtype)


def _merge(ya, yb, yc, proj2, wb_bf, l):
    n = ya.shape[0]
    tm = _pick(n, 512)
    tn = 1024
    ybs = pl.BlockSpec((tm, A_WIDTH), lambda j, i: (i, 0))
    gate = lambda b: pl.BlockSpec((tm, tn), lambda j, i: (i, (OFF_GATE + b * D_MODEL) // tn + j))
    return pl.pallas_call(
        _merge_kernel,
        grid=(D_MODEL // tn, n // tm),
        in_specs=[ybs, ybs, ybs, gate(0), gate(1), gate(2),
                  pl.BlockSpec((None, 3, A_WIDTH, tn), lambda j, i: (l, 0, 0, j))],
        out_specs=pl.BlockSpec((tm, tn), lambda j, i: (i, j)),
        out_shape=jax.ShapeDtypeStruct((n, D_MODEL), BF16),
        compiler_params=_cparams(("parallel", "parallel"), 48),
    )(ya, yb, yc, proj2, proj2, proj2, wb_bf)


ROUTE_GROUP_LANE = N_EXPERTS


def _outproj_router_kernel(u_ref, w_ref, x_ref, nw_ref, wcat_ref, br_ref, xo_ref, o_ref, ot_ref, cnt_ref, run_scr):
    @pl.when(pl.program_id(0) == 0)
    def _():
        run_scr[...] = jnp.zeros_like(run_scr)

    tm = x_ref.shape[0]
    x_new = x_ref[...] + jnp.dot(u_ref[...], w_ref[...], preferred_element_type=F32)
    xo_ref[...] = x_new
    xn = _rms(x_new, nw_ref[...])
    hi = xn.astype(BF16)
    lo = (xn - hi.astype(F32)).astype(BF16)
    both = jnp.dot(hi, wcat_ref[...], preferred_element_type=F32)
    logit = (both[:, :LANES] + both[:, LANES:]
             + jnp.dot(lo, wcat_ref[:, :LANES], preferred_element_type=F32) + br_ref[...])
    lane = lax.broadcasted_iota(jnp.int32, logit.shape, 1).astype(F32)
    big = 1e9
    is_g = (lane >= ROUTE_GROUP_LANE) & (lane < ROUTE_GROUP_LANE + N_GROUPS)
    lg = jnp.where(is_g, logit, NEG_BIG)
    mg = jnp.max(lg, axis=-1, keepdims=True)
    g_sel = jnp.min(jnp.where(lg == mg, lane - ROUTE_GROUP_LANE, big), axis=-1, keepdims=True)
    p_group = 1.0 / jnp.sum(jnp.exp(lg - mg), axis=-1, keepdims=True)
    lo_lane = g_sel * EXPERTS_PER_GROUP
    in_g = (lane >= lo_lane) & (lane < lo_lane + EXPERTS_PER_GROUP)
    le = jnp.where(in_g, logit, NEG_BIG)
    v1 = jnp.max(le, axis=-1, keepdims=True)
    i1 = jnp.min(jnp.where(le == v1, lane, big), axis=-1, keepdims=True)
    le2 = jnp.where(lane == i1, NEG_BIG, le)
    v2 = jnp.max(le2, axis=-1, keepdims=True)
    i2 = jnp.min(jnp.where(le2 == v2, lane, big), axis=-1, keepdims=True)
    e2 = jnp.exp(v2 - v1)
    den = 1.0 + e2
    p1 = p_group / den
    p2 = p_group * e2 / den
    oh1 = jnp.where(lane == i1, 1.0, 0.0)
    oh2 = jnp.where(lane == i2, 1.0, 0.0)
    oh = oh1 + oh2
    row = lax.broadcasted_iota(jnp.int32, (tm, tm), 0)
    col = lax.broadcasted_iota(jnp.int32, (tm, tm), 1)
    tri = jnp.where(row > col, 1.0, 0.0).astype(BF16)
    before = run_scr[...] + jnp.dot(tri, oh.astype(BF16), preferred_element_type=F32)
    r1 = jnp.sum(oh1 * before, axis=-1, keepdims=True)
    r2 = jnp.sum(oh2 * before, axis=-1, keepdims=True)
    run = run_scr[...] + jnp.sum(oh, axis=0, keepdims=True)
    run_scr[...] = run
    cnt_ref[...] = jnp.broadcast_to(run, cnt_ref.shape)
    out = jnp.zeros_like(logit)
    for n, val in enumerate((i1, i2, p1, p2, r1, r2)):
        out = jnp.where(lane == n, val, out)
    o_ref[...] = out
    ot_ref[...] = jnp.transpose(out)[:8]


def _outproj_router(u, w_bf, x2, l, nw, w_rg, b_rg, w_re, b_re):
    n = x2.shape[0]
    tm = _pick(n, 512)
    w = jnp.zeros((D_MODEL, LANES), F32).at[:, :N_EXPERTS].set(w_re).at[:, N_EXPERTS:N_EXPERTS + N_GROUPS].set(w_rg)
    br = jnp.zeros((1, LANES), F32).at[0, :N_EXPERTS].set(b_re).at[0, N_EXPERTS:N_EXPERTS + N_GROUPS].set(b_rg)
    whi = w.astype(BF16)
    wlo = (w - whi.astype(F32)).astype(BF16)
    full = lambda shape: pl.BlockSpec(shape, lambda i: (0, 0))
    rows = pl.BlockSpec((tm, D_MODEL), lambda i: (i, 0))
    return pl.pallas_call(
        _outproj_router_kernel,
        grid=(n // tm,),
        in_specs=[rows, pl.BlockSpec((None, D_MODEL, D_MODEL), lambda i: (l, 0, 0)), rows,
                  full((1, D_MODEL)), full((D_MODEL, 2 * LANES)), full((1, LANES))],
        out_specs=[rows, pl.BlockSpec((tm, LANES), lambda i: (i, 0)), pl.BlockSpec((8, tm), lambda i: (0, i)),
                   full((8, LANES))],
        out_shape=[jax.ShapeDtypeStruct((n, D_MODEL), F32), jax.ShapeDtypeStruct((n, LANES), F32),
                   jax.ShapeDtypeStruct((8, n), F32), jax.ShapeDtypeStruct((8, LANES), F32)],
        scratch_shapes=[pltpu.VMEM((1, LANES), F32)],
        compiler_params=_cparams(("arbitrary",), 48),
    )(u, w_bf, x2, nw.reshape(1, D_MODEL), jnp.concatenate([whi, wlo], axis=1), br)


def _moe_plan(e1, e2, r1, r2, cnt, tm_e):
    n = e1.shape[0]
    ar = jnp.arange(N_EXPERTS, dtype=jnp.int32)
    offs = jnp.cumsum(cnt) - cnt
    dest = jnp.stack([jnp.take(offs, e1) + r1, jnp.take(offs, e2) + r2])
    n_tiles = (2 * n) // tm_e
    first = offs // tm_e
    last = jnp.where(cnt > 0, (offs + cnt - 1) // tm_e, first - 1)
    n_items = last - first + 1
    item_end = jnp.cumsum(n_items)
    item_start = item_end - n_items
    n_work = n_tiles + N_EXPERTS - 1
    w = jnp.arange(n_work, dtype=jnp.int32)
    e_w = jnp.minimum(jnp.sum((w[:, None] >= item_end[None, :]).astype(jnp.int32), axis=1), N_EXPERTS - 1)
    valid = w < item_end[-1]
    ohw = (e_w[:, None] == ar[None, :]).astype(jnp.int32)
    sel = lambda tab: jnp.sum(ohw * tab[None, :], axis=1)
    tile = sel(first) + (w - sel(item_start))
    lo = jnp.maximum(sel(offs), tile * tm_e)
    hi = jnp.minimum(sel(offs) + sel(cnt), (tile + 1) * tm_e)
    last_e = jnp.max(jnp.where(n_items > 0, ar, 0))
    tile = jnp.where(valid, tile, n_tiles - 1)
    e_w = jnp.where(valid, e_w, last_e)
    lo = jnp.where(valid, lo, 0)
    hi = jnp.where(valid, hi, 0)
    return dest.astype(jnp.int32), tile.astype(jnp.int32), e_w.astype(jnp.int32), lo.astype(jnp.int32), hi.astype(jnp.int32)


def _row_copy(src_ref, src_row, dst_ref, dst_row, sem):
    return pltpu.make_async_copy(src_ref.at[pl.ds(src_row, 1), :], dst_ref.at[pl.ds(dst_row, 1), :], sem)


ROW_UNROLL = 8


def _dispatch_kernel(dest_ref, x_ref, xs_ref, sem, *, tm):
    def copies(i):
        for u in range(ROW_UNROLL):
            r = i * ROW_UNROLL + u
            for k in range(2):
                yield _row_copy(x_ref, r, xs_ref, dest_ref[k, r], sem)

    def issue(i, carry):
        for n, cp in enumerate(copies(i)):
            cp.start(priority=n % 2)
        return carry

    def drain(i, carry):
        for cp in copies(i):
            cp.wait()
        return carry

    lax.fori_loop(0, tm // ROW_UNROLL, issue, 0)
    lax.fori_loop(0, tm // ROW_UNROLL, drain, 0)


def _dispatch(x2, dest3, tm):
    n = x2.shape[0]
    return pl.pallas_call(
        functools.partial(_dispatch_kernel, tm=tm),
        grid=(n // tm,),
        in_specs=[pl.BlockSpec((None, 2, tm), lambda i: (i, 0, 0), memory_space=pltpu.SMEM),
                  pl.BlockSpec((tm, D_MODEL), lambda i: (i, 0))],
        out_specs=pl.BlockSpec(memory_space=pl.ANY),
        out_shape=jax.ShapeDtypeStruct((2 * n, D_MODEL), F32),
        scratch_shapes=[pltpu.SemaphoreType.DMA(())],
        compiler_params=_cparams(("arbitrary",), 32),
    )(dest3, x2)


def _experts_kernel(tile_ref, exp_ref, lo_ref, hi_ref, xs_ref, nw_ref, wg_ref, wu_ref, wd_ref, ys_ref, *, tm):
    w = pl.program_id(0)
    lo = lo_ref[w]
    hi = hi_ref[w]
    tile = tile_ref[w]
    first_visit = jnp.logical_or(w == 0, tile_ref[jnp.maximum(w - 1, 0)] != tile)

    @pl.when(hi > lo)
    def _():
        xn = _rms(xs_ref[...], nw_ref[...]).astype(BF16)
        gate = jnp.dot(xn, wg_ref[...].astype(BF16), preferred_element_type=F32)
        up = jnp.dot(xn, wu_ref[...].astype(BF16), preferred_element_type=F32)
        h = (gate * jax.nn.sigmoid(gate) * up).astype(BF16)
        y = jnp.dot(h, wd_ref[...].astype(BF16), preferred_element_type=F32)

        @pl.when(first_visit)
        def _():
            ys_ref[...] = y

        @pl.when(jnp.logical_not(first_visit))
        def _():
            rows = tile * tm + lax.broadcasted_iota(jnp.int32, (tm, 1), 0)
            ys_ref[...] = jnp.where((rows >= lo) & (rows < hi), y, ys_ref[...])


def _experts(xs, nw, wg_bf, wu_bf, wd_bf, l, tile, exp, lo, hi, tm):
    n_work = tile.shape[0]
    grid_spec = pltpu.PrefetchScalarGridSpec(
        num_scalar_prefetch=4,
        grid=(n_work,),
        in_specs=[pl.BlockSpec((tm, D_MODEL), lambda w, t, e, lo, hi: (t[w], 0)),
                  pl.BlockSpec((1, D_MODEL), lambda w, t, e, lo, hi: (0, 0)),
                  pl.BlockSpec((None, None, D_MODEL, D_EXPERT), lambda w, t, e, lo, hi: (l, e[w], 0, 0)),
                  pl.BlockSpec((None, None, D_MODEL, D_EXPERT), lambda w, t, e, lo, hi: (l, e[w], 0, 0)),
                  pl.BlockSpec((None, None, D_EXPERT, D_MODEL), lambda w, t, e, lo, hi: (l, e[w], 0, 0))],
        out_specs=pl.BlockSpec((tm, D_MODEL), lambda w, t, e, lo, hi: (t[w], 0)),
    )
    return pl.pallas_call(
        functools.partial(_experts_kernel, tm=tm),
        grid_spec=grid_spec,
        out_shape=jax.ShapeDtypeStruct(xs.shape, F32),
        compiler_params=_cparams(("arbitrary",), 56),
    )(tile, exp, lo, hi, xs, nw.reshape(1, D_MODEL), wg_bf, wu_bf, wd_bf)


def _combine_kernel(dest_ref, route_ref, x_ref, ys_ref, o_ref, buf0, buf1, sems, *, tm, n_tok):
    i = pl.program_id(0)

    def copies(step, slot, j):
        for u in range(ROW_UNROLL):
            r = j * ROW_UNROLL + u
            tok = step * tm + r
            yield _row_copy(ys_ref, dest_ref[tok], buf0.at[slot], r, sems.at[slot])
            yield _row_copy(ys_ref, dest_ref[n_tok + tok], buf1.at[slot], r, sems.at[slot])

    def issue(step, slot):
        def body(j, carry):
            for n, cp in enumerate(copies(step, slot, j)):
                cp.start(priority=n % 2)
            return carry

        lax.fori_loop(0, tm // ROW_UNROLL, body, 0)

    def drain(step, slot):
        def body(j, carry):
            for cp in copies(step, slot, j):
                cp.wait()
            return carry

        lax.fori_loop(0, tm // ROW_UNROLL, body, 0)

    slot = i % 2

    @pl.when(i == 0)
    def _():
        issue(0, 0)

    @pl.when(i + 1 < pl.num_programs(0))
    def _():
        issue(i + 1, 1 - slot)

    drain(i, slot)
    route = route_ref[...]
    o_ref[...] = x_ref[...] + route[:, 2:3] * buf0[slot] + route[:, 3:4] * buf1[slot]


def _combine(x2, route, ys, dest, tm):
    n = x2.shape[0]
    grid_spec = pltpu.PrefetchScalarGridSpec(
        num_scalar_prefetch=1,
        grid=(n // tm,),
        in_specs=[pl.BlockSpec((tm, LANES), lambda i, d: (i, 0)),
                  pl.BlockSpec((tm, D_MODEL), lambda i, d: (i, 0)),
                  pl.BlockSpec(memory_space=pl.ANY)],
        out_specs=pl.BlockSpec((tm, D_MODEL), lambda i, d: (i, 0)),
        scratch_shapes=[pltpu.VMEM((2, tm, D_MODEL), F32), pltpu.VMEM((2, tm, D_MODEL), F32),
                        pltpu.SemaphoreType.DMA((2,))],
    )
    return pl.pallas_call(
        functools.partial(_combine_kernel, tm=tm, n_tok=n),
        grid_spec=grid_spec,
        out_shape=jax.ShapeDtypeStruct((n, D_MODEL), F32),
        compiler_params=_cparams(("arbitrary",), 48),
    )(dest.reshape(2 * n), route, x2, ys)


def _ffn(x2, route, route_t, cnt, lw, l):
    n = x2.shape[0]
    fields = route_t.astype(jnp.int32)
    tm_e = _pick(2 * n, 512) if n >= 4096 else 128
    tm_r = _pick(n, 512)
    dest, tile, exp, lo, hi = _moe_plan(fields[0], fields[1], fields[4], fields[5],
                                        cnt[0, :N_EXPERTS].astype(jnp.int32), tm_e)
    dest3 = dest.reshape(2, n // tm_r, tm_r).transpose(1, 0, 2)
    xs = _dispatch(x2, dest3, tm_r)
    ys = _experts(xs, lw["norm_ffn_w"], lw["wg"], lw["wu"], lw["wd"], l, tile, exp, lo, hi, tm_e)
    return _combine(x2, route, ys, dest, tm_r)


def _layer(x3, pos, l, depth, lw, ret_state, kv_cache, prev_kv):
    bsz, seq, _ = x3.shape
    n = bsz * seq
    sample = ret_state is not None
    lam_init = 0.8 - 0.6 * math.exp(-0.3 * l)
    x2 = x3.reshape(n, D_MODEL)
    proj2 = _inproj(x2, lw["norm_mix_w"], lw["w_in"], l)
    proj3 = proj2.reshape(bsz, seq, D_IN)

    a_y, a_v = _gmlp(proj2, seq, lw["a_norm_w"], lw["a_ws"], lw["a_bs_b"], emit_v=sample)
    b_y, s_new = _retention(proj3, pos, lw["b_norm_w"], ret_state)
    prep = _cprep(proj3, lw["c_qnorm_w"], lw["c_knorm_w"], lw["c_lambda"], lam_init, not sample, l, depth, prev_kv)
    qn, kn, kf_all, vf_all, lam_b = prep[:5]
    if sample:
        c_y = _dattn_sample(qn, kn, proj3, kv_cache[0], kv_cache[1], l, lam_b, lw["c_subln_w"], 1.0 - lam_init)
    else:
        c_y = _dattn_prompt(qn, kn, prep[5], lam_b, lw["c_subln_w"], 1.0 - lam_init)

    u = _merge(a_y, b_y.reshape(n, A_WIDTH), c_y.reshape(n, A_WIDTH), proj2, lw["w_branch"], l)
    x2, route, route_t, cnt = _outproj_router(u, lw["w_out"], x2, l, lw["norm_ffn_w"], lw["w_rg"], lw["b_rg"],
                                              lw["w_re"], lw["b_re"])
    x2 = _ffn(x2, route, route_t, cnt, lw, l)
    return x2.reshape(bsz, seq, D_MODEL), (kf_all, vf_all), s_new, a_v


def kernel(x_prompt, x_sample, cache_k_c, cache_v_c, state_ret, norm_mix_w, w_in, a_norm_w, a_ws, a_bs, b_norm_w, c_qnorm_w, c_knorm_w, c_lambda, c_subln_w, w_branch, w_out, norm_ffn_w, w_router_group, b_router_group, w_router_expert, b_router_expert, w_gate_e, w_up_e, w_down_e):
    depth = w_in.shape[0]
    past = cache_k_c.shape[2]
    pos_p = jnp.arange(x_prompt.shape[1])
    pos_s = past + jnp.arange(x_sample.shape[1])
    yp, ys = x_prompt, x_sample
    big = dict(w_in=w_in.astype(BF16), w_branch=w_branch.astype(BF16), w_out=w_out.astype(BF16),
               wg=w_gate_e, wu=w_up_e, wd=w_down_e)
    kv_p = kv_s = None
    rets_p, rets_s, avs = [], [], []
    for l in range(depth):
        lw = dict(
            big, norm_mix_w=norm_mix_w[l], a_norm_w=a_norm_w[l], a_ws=a_ws[l],
            a_bs_b=jnp.broadcast_to(a_bs[l][:, :, None], (A_GROUPS, A_BLOCK, A_GROUP)),
            b_norm_w=b_norm_w[l], c_qnorm_w=c_qnorm_w[l], c_knorm_w=c_knorm_w[l], c_lambda=c_lambda[l],
            c_subln_w=c_subln_w[l], norm_ffn_w=norm_ffn_w[l], w_rg=w_router_group[l], b_rg=b_router_group[l],
            w_re=w_router_expert[l], b_re=b_router_expert[l])
        yp, kv_p, rp, _ = _layer(yp, pos_p, l, depth, lw, None, None, kv_p)
        ys, kv_s, rn, avn = _layer(ys, pos_s, l, depth, lw, state_ret[l], (cache_k_c, cache_v_c), kv_s)
        rets_p.append(rp)
        rets_s.append(rn)
        avs.append(avn.reshape(ys.shape[0], ys.shape[1], A_WIDTH))
    as_cache = lambda buf, x: buf.reshape(depth, x.shape[0], x.shape[1], C_HEADS, LANES)
    return (yp, ys, as_cache(kv_p[0], x_prompt), as_cache(kv_p[1], x_prompt), jnp.stack(rets_p, 0),
            as_cache(kv_s[0], x_sample), as_cache(kv_s[1], x_sample), jnp.stack(rets_s, 0), jnp.stack(avs, 0))
```
